```python
import jax
import jax.numpy as jnp
from jax import lax
import numpy as np

D_MODEL = 2048
BATCH = 4
SEQ = 2048
DEPTH = 4
DEC_BATCH = 128
DEC_SEQ = 4
PAST_LEN = 8192
PAGE_SIZE = 128

N_MIXERS = 4
N_RET_LAYERS = (DEPTH + 3) // 4
N_CM_LAYERS = (DEPTH + 2) // 4
N_DSA_LAYERS = (DEPTH + 1) // 4
N_MLA_LAYERS = DEPTH // 4

EPS = 1e-6
Q_BLOCK = 128
RET_HEADS = 8
RET_DK = D_MODEL // RET_HEADS
RET_DV = 2 * D_MODEL // RET_HEADS
RET_CHUNK = 128
RET_THETA = 10000.0
CM_CHUNK = 128
CM_WIDTH = 3 * D_MODEL
CM_GROUPS = 8
DSA_HEADS = 16
DSA_KV_HEADS = 4
DSA_HEAD_DIM = D_MODEL // DSA_HEADS
DSA_REP = DSA_HEADS // DSA_KV_HEADS
IDX_HEADS = 16
IDX_DIM = 128
IDX_TOPK = 256
ROPE_THETA = 500000.0
ROPE_DIMS = DSA_HEAD_DIM // 4
IDX_ROPE_DIMS = IDX_DIM // 4
MLA_HEADS = 16
MLA_Q_LORA = 3 * D_MODEL // 8
MLA_KV_LORA = D_MODEL // 4
MLA_NOPE = 128
MLA_ROPE = 64
MLA_VDIM = 128
MLA_THETA = 10000.0
MOE_GROUPS = 8
MOE_EPG = 8
MOE_EXPERTS = MOE_GROUPS * MOE_EPG
MOE_TOPK = 2
MOE_FF = D_MODEL // 4
MOE_BLOCK = 64

F32 = jnp.float32

kernel_name = "hybrid_retention_gmlp_dsa_mla_hmoe_step"


def split_last(x, sizes):
    out, start = [], 0
    for s in sizes:
        out.append(x[..., start:start + s])
        start += s
    return out


def rms_norm(x, g):
    xf = x.astype(F32)
    y = xf * lax.rsqrt(jnp.mean(xf * xf, -1, keepdims=True) + EPS)
    return (y * g.astype(F32)).astype(x.dtype)


def rope(x, pos, theta, n_rot):
    half = n_rot // 2
    inv = theta ** (-jnp.arange(half, dtype=F32) / half)
    ang = pos.astype(F32)[:, None] * inv[None, :]
    cos, sin = jnp.cos(ang)[:, None, :], jnp.sin(ang)[:, None, :]
    xr = x[..., :n_rot].astype(F32)
    x1, x2 = xr[..., :half], xr[..., half:]
    rot = jnp.concatenate([x1 * cos - x2 * sin, x2 * cos + x1 * sin], -1).astype(x.dtype)
    return jnp.concatenate([rot, x[..., n_rot:]], -1)


def retention_mixer(x, pos, state0, w_in, gn, w_o):
    B, L, _ = x.shape
    q, k, v, g = split_last(x @ w_in, [RET_HEADS * RET_DK, RET_HEADS * RET_DK,
                                       RET_HEADS * RET_DV, RET_HEADS * RET_DV])
    q = rope(q.reshape(B, L, RET_HEADS, RET_DK), pos, RET_THETA, RET_DK)
    k = rope(k.reshape(B, L, RET_HEADS, RET_DK), pos, RET_THETA, RET_DK) * (RET_DK ** -0.5)
    v = v.reshape(B, L, RET_HEADS, RET_DV)
    C = RET_CHUNK if L % RET_CHUNK == 0 else L
    n = L // C
    log_gamma = jnp.log1p(-(2.0 ** (-5.0 - jnp.arange(RET_HEADS, dtype=F32))))
    idx = jnp.arange(C, dtype=F32)
    rel = idx[:, None] - idx[None, :]
    d_inner = jnp.where(rel >= 0, jnp.exp(log_gamma[:, None, None] * jnp.maximum(rel, 0.0)), 0.0)
    d_query = jnp.exp((idx[:, None] + 1.0) * log_gamma[None, :])[None, :, :, None]
    d_key = jnp.exp((C - 1.0 - idx[:, None]) * log_gamma[None, :])[None, :, :, None]
    d_chunk = jnp.exp(C * log_gamma)[None, :, None, None]

    def chunks(t):
        return jnp.moveaxis(t.astype(F32).reshape(B, n, C, RET_HEADS, t.shape[-1]), 1, 0)

    def step(S, qkv):
        qc, kc, vc = qkv
        att = jnp.einsum('bihd,bjhd->bhij', qc, kc) * d_inner
        o = jnp.einsum('bhij,bjhe->bihe', att, vc) + jnp.einsum('bihd,bhde->bihe', qc, S) * d_query
        S = S * d_chunk + jnp.einsum('bjhd,bjhe->bhde', kc * d_key, vc)
        return S, o

    S, o = lax.scan(step, state0.astype(F32), (chunks(q), chunks(k), chunks(v)))
    o = jnp.moveaxis(o, 0, 1).reshape(B, L, RET_HEADS, RET_DV)
    mu = jnp.mean(o, -1, keepdims=True)
    var = jnp.mean(jnp.square(o - mu), -1, keepdims=True)
    o = (o - mu) * lax.rsqrt(var + EPS) * gn.astype(F32)
    o = o.reshape(B, L, RET_HEADS * RET_DV).astype(x.dtype) * jax.nn.silu(g)
    return o @ w_o, S.astype(state0.dtype)


def chunk_mlp_mixer(x, w_in, ln_g, ln_b, w_s, b_s, w_o):
    B, L, _ = x.shape
    C = CM_CHUNK if L % CM_CHUNK == 0 else L
    n = L // C
    u, v = split_last(jax.nn.gelu(x @ w_in), [CM_WIDTH, CM_WIDTH])
    vf = v.astype(F32)
    mu = jnp.mean(vf, -1, keepdims=True)
    var = jnp.mean(jnp.square(vf - mu), -1, keepdims=True)
    v = ((vf - mu) * lax.rsqrt(var + EPS) * ln_g.astype(F32) + ln_b.astype(F32)).astype(x.dtype)
    causal = jnp.tril(jnp.ones((C, C), dtype=bool))
    ws = jnp.where(causal[None], w_s[:, :C, :C], 0).astype(x.dtype)
    vg = v.reshape(B, n, C, CM_GROUPS, CM_WIDTH // CM_GROUPS)
    s = jnp.einsum('gts,bnsgc->bntgc', ws, vg) + b_s[:, :C].T.astype(x.dtype)[None, None, :, :, None]
    y = u * s.reshape(B, L, CM_WIDTH)
    return y @ w_o, v


def dsa_project(x, pos, w_in, q_norm, k_norm):
    B, L, _ = x.shape
    q, k, v, qi, ki, wh = split_last(x @ w_in, [DSA_HEADS * DSA_HEAD_DIM, DSA_KV_HEADS * DSA_HEAD_DIM,
                                                DSA_KV_HEADS * DSA_HEAD_DIM, IDX_HEADS * IDX_DIM,
                                                IDX_DIM, IDX_HEADS])
    q = rope(rms_norm(q.reshape(B, L, DSA_HEADS, DSA_HEAD_DIM), q_norm), pos, ROPE_THETA, ROPE_DIMS)
    k = rope(rms_norm(k.reshape(B, L, DSA_KV_HEADS, DSA_HEAD_DIM), k_norm), pos, ROPE_THETA, ROPE_DIMS)
    v = v.reshape(B, L, DSA_KV_HEADS, DSA_HEAD_DIM)
    qi = rope(qi.reshape(B, L, IDX_HEADS, IDX_DIM), pos, ROPE_THETA, IDX_ROPE_DIMS)
    ki = rope(ki.reshape(B, L, 1, IDX_DIM), pos, ROPE_THETA, IDX_ROPE_DIMS)[:, :, 0]
    return q, k, v, qi, ki, wh * (IDX_HEADS ** -0.5)


def indexer_scores(qi, wh, ki, qpos):
    dots = jnp.einsum('bthd,bsd->bths', qi, ki).astype(F32) * (IDX_DIM ** -0.5)
    sc = jnp.einsum('bth,bths->bts', wh.astype(F32), jax.nn.relu(dots))
    spos = jnp.arange(ki.shape[1])
    return jnp.where(spos[None, None, :] <= qpos[None, :, None], sc, -jnp.inf)


def sparse_attend(q, qpos, scores, fetch_kv, k_top):
    B, T = q.shape[:2]
    _, sel = lax.top_k(scores, k_top)
    valid = sel <= qpos[None, :, None]
    ks, vs = fetch_kv(sel)
    qg = q.reshape(B, T, DSA_KV_HEADS, DSA_REP, DSA_HEAD_DIM)
    s = jnp.einsum('btgrd,btkgd->btgrk', qg, ks).astype(F32) * (DSA_HEAD_DIM ** -0.5)
    s = jnp.where(valid[:, :, None, None, :], s, -jnp.inf)
    p = jax.nn.softmax(s, -1).astype(q.dtype)
    o = jnp.einsum('btgrk,btkgd->btgrd', p, vs)
    return o.reshape(B, T, DSA_HEADS * DSA_HEAD_DIM)


def dsa_prompt(x, w_in, q_norm, k_norm, w_o):
    B, L, _ = x.shape
    q, k, v, qi, ki, wh = dsa_project(x, jnp.arange(L), w_in, q_norm, k_norm)
    k_top = min(IDX_TOPK, L // 4)
    take_rows = jax.vmap(lambda rows, sel: rows[sel])

    def fetch(sel):
        return take_rows(k, sel), take_rows(v, sel)

    def block(b):
        s0 = b * Q_BLOCK
        qpos = s0 + jnp.arange(Q_BLOCK)
        cut = lambda t: lax.dynamic_slice_in_dim(t, s0, Q_BLOCK, axis=1)
        scores = indexer_scores(cut(qi), cut(wh), ki, qpos)
        return sparse_attend(cut(q), qpos, scores, fetch, k_top)

    o = lax.map(block, jnp.arange(L // Q_BLOCK))
    o = jnp.moveaxis(o, 0, 1).reshape(B, L, DSA_HEADS * DSA_HEAD_DIM)
    return o @ w_o, k, v, ki


def dsa_sample(x, cache_k, cache_v, cache_ki, page_table, w_in, q_norm, k_norm, w_o):
    B, T, _ = x.shape
    past = page_table.shape[1] * PAGE_SIZE
    pos = past + jnp.arange(T)
    q, k, v, qi, ki, wh = dsa_project(x, pos, w_in, q_norm, k_norm)
    ki_past = cache_ki[page_table].reshape(B, past, IDX_DIM).astype(ki.dtype)
    scores = indexer_scores(qi, wh, jnp.concatenate([ki_past, ki], 1), pos)
    k_top = min(IDX_TOPK, (past + T) // 4)
    bidx = jnp.arange(B)[:, None, None]

    def fetch(sel):
        in_past = (sel < past)[..., None, None]
        ps = jnp.minimum(sel, past - 1)
        phys = page_table[bidx, ps // PAGE_SIZE]
        off = ps % PAGE_SIZE
        ns = jnp.clip(sel - past, 0, T - 1)
        k_sel = jnp.where(in_past, cache_k[phys, off].astype(k.dtype), k[bidx, ns])
        v_sel = jnp.where(in_past, cache_v[phys, off].astype(v.dtype), v[bidx, ns])
        return k_sel, v_sel

    o = sparse_attend(q, pos, scores, fetch, k_top)
    return o @ w_o, k, v, ki


def mla_project(x, pos, w_in, qa_norm, kv_norm, w_uq, qn_nope, qn_pe, kpe_norm):
    B, L, _ = x.shape
    cq, ckv, kpe = split_last(x @ w_in, [MLA_Q_LORA, MLA_KV_LORA, MLA_ROPE])
    q = (rms_norm(cq, qa_norm) @ w_uq).reshape(B, L, MLA_HEADS, MLA_NOPE + MLA_ROPE)
    q_nope = rms_norm(q[..., :MLA_NOPE], qn_nope)
    q_pe = rope(rms_norm(q[..., MLA_NOPE:], qn_pe), pos, MLA_THETA, MLA_ROPE)
    ckv = rms_norm(ckv, kv_norm)
    kpe = rope(rms_norm(kpe, kpe_norm)[:, :, None, :], pos, MLA_THETA, MLA_ROPE)[:, :, 0]
    return q_nope, q_pe, ckv, kpe


def mla_prompt(x, w_in, qa_norm, kv_norm, w_uq, qn_nope, qn_pe, kpe_norm, w_uk, w_uv, w_o):
    B, L, _ = x.shape
    q_nope, q_pe, ckv, kpe = mla_project(x, jnp.arange(L), w_in, qa_norm, kv_norm, w_uq,
                                         qn_nope, qn_pe, kpe_norm)
    k_nope = jnp.einsum('bsc,chn->bshn', ckv, w_uk)
    v = jnp.einsum('bsc,chv->bshv', ckv, w_uv)
    scale = (MLA_NOPE + MLA_ROPE) ** -0.5
    spos = jnp.arange(L)

    def block(b):
        s0 = b * Q_BLOCK
        qpos = s0 + jnp.arange(Q_BLOCK)
        cut = lambda t: lax.dynamic_slice_in_dim(t, s0, Q_BLOCK, axis=1)
        s = (jnp.einsum('bthn,bshn->bhts', cut(q_nope), k_nope)
             + jnp.einsum('bthr,bsr->bhts', cut(q_pe), kpe)).astype(F32) * scale
        s = jnp.where(spos[None, :] <= qpos[:, None], s, -jnp.inf)
        p = jax.nn.softmax(s, -1).astype(x.dtype)
        return jnp.einsum('bhts,bshv->bthv', p, v)

    o = lax.map(block, jnp.arange(L // Q_BLOCK))
    o = jnp.moveaxis(o, 0, 1).reshape(B, L, MLA_HEADS * MLA_VDIM)
    return o @ w_o, ckv, kpe


def mla_sample(x, cache_ckv, cache_kpe, page_table, w_in, qa_norm, kv_norm, w_uq, qn_nope, qn_pe,
               kpe_norm, w_uk, w_uv, w_o):
    B, T, _ = x.shape
    past = page_table.shape[1] * PAGE_SIZE
    pos = past + jnp.arange(T)
    q_nope, q_pe, ckv, kpe = mla_project(x, pos, w_in, qa_norm, kv_norm, w_uq, qn_nope, qn_pe, kpe_norm)
    q_lat = jnp.einsum('bthn,chn->bthc', q_nope, w_uk)
    scale = (MLA_NOPE + MLA_ROPE) ** -0.5

    def logits(c, r):
        return (jnp.einsum('bthc,bsc->bths', q_lat, c.astype(q_lat.dtype))
                + jnp.einsum('bthr,bsr->bths', q_pe, r.astype(q_pe.dtype))).astype(F32) * scale

    tq = jnp.arange(T)
    s = jnp.where((tq[None, :] <= tq[:, None])[None, :, None, :], logits(ckv, kpe), -jnp.inf)
    m = jnp.max(s, -1)
    p = jnp.exp(s - m[..., None])
    l = jnp.sum(p, -1)
    acc = jnp.einsum('bths,bsc->bthc', p, ckv.astype(F32))

    def page_step(carry, phys):
        m, l, acc = carry
        c = cache_ckv[phys]
        s = logits(c, cache_kpe[phys])
        m_new = jnp.maximum(m, jnp.max(s, -1))
        a = jnp.exp(m - m_new)
        p = jnp.exp(s - m_new[..., None])
        l = l * a + jnp.sum(p, -1)
        acc = acc * a[..., None] + jnp.einsum('bths,bsc->bthc', p, c.astype(F32))
        return (m_new, l, acc), None

    (m, l, acc), _ = lax.scan(page_step, (m, l, acc), page_table.T)
    o_lat = (acc / l[..., None]).astype(x.dtype)
    o = jnp.einsum('bthc,chv->bthv', o_lat, w_uv).reshape(B, T, MLA_HEADS * MLA_VDIM)
    return o @ w_o, ckv, kpe


def hier_moe(x, w_group, b_group, w_expert, b_expert, w1, w3, w2):
    shape = x.shape
    xt = x.reshape(-1, shape[-1])
    N = xt.shape[0]
    tok = jnp.arange(N)
    lg = (xt @ w_group).astype(F32) + b_group.astype(F32)
    grp = jnp.argmax(lg, -1)
    p_grp = jax.nn.softmax(lg, -1)[tok, grp]
    le = ((xt @ w_expert).astype(F32) + b_expert.astype(F32)).reshape(N, MOE_GROUPS, MOE_EPG)[tok, grp]
    top_v, top_i = lax.top_k(le, MOE_TOPK)
    gates = (p_grp[:, None] * jax.nn.softmax(top_v, -1)).reshape(-1)
    experts = (grp[:, None] * MOE_EPG + top_i).reshape(-1)
    tok_flat = jnp.repeat(tok, MOE_TOPK)
    A = N * MOE_TOPK
    order = jnp.argsort(experts)
    e_s, t_s, g_s = experts[order], tok_flat[order], gates[order]
    counts = jnp.bincount(experts, length=MOE_EXPERTS)
    starts = jnp.cumsum(counts) - counts
    pcounts = (counts + MOE_BLOCK - 1) // MOE_BLOCK * MOE_BLOCK
    pends = jnp.cumsum(pcounts)
    pstarts = pends - pcounts
    dest = pstarts[e_s] + jnp.arange(A) - starts[e_s]
    n_blocks = -(-A // MOE_BLOCK) + MOE_EXPERTS
    P = n_blocks * MOE_BLOCK
    tok_pad = jnp.full((P,), N, jnp.int32).at[dest].set(t_s.astype(jnp.int32))
    gate_pad = jnp.zeros((P,), F32).at[dest].set(g_s)
    blk_e = jnp.minimum(jnp.searchsorted(pends, jnp.arange(n_blocks) * MOE_BLOCK, side='right'),
                        MOE_EXPERTS - 1)
    x_ext = jnp.concatenate([xt, jnp.zeros((1, shape[-1]), xt.dtype)], 0)
    xb = x_ext[tok_pad].reshape(n_blocks, MOE_BLOCK, shape[-1])

    def expert_block(args):
        rows, e = args
        h = jax.nn.silu(rows @ w1[e]) * (rows @ w3[e])
        return h @ w2[e]

    yb = lax.map(expert_block, (xb, blk_e)).reshape(P, shape[-1])
    y = jax.ops.segment_sum(yb.astype(F32) * gate_pad[:, None], tok_pad, num_segments=N + 1)[:N]
    return y.astype(x.dtype).reshape(shape)


def setup_inputs(seed: int = 0) -> dict:
    key = jax.random.key(seed)
    keys = iter(jax.random.split(key, 64))
    D = D_MODEL

    def normal(shape, scale):
        return jax.random.normal(next(keys), shape, F32) * scale

    def gain(shape):
        return 1.0 + 0.05 * jax.random.normal(next(keys), shape, F32)

    n_pages = PAST_LEN // PAGE_SIZE
    n_used = DEC_BATCH * n_pages
    n_pool = n_used + max(1, n_used // 4)
    page_table = jax.random.permutation(next(keys), n_pool)[:n_used].reshape(DEC_BATCH, n_pages).astype(jnp.int32)
    ret_in = 2 * RET_HEADS * RET_DK + 2 * RET_HEADS * RET_DV
    dsa_in = (DSA_HEADS * DSA_HEAD_DIM + 2 * DSA_KV_HEADS * DSA_HEAD_DIM
              + IDX_HEADS * IDX_DIM + IDX_DIM + IDX_HEADS)
    mla_in = MLA_Q_LORA + MLA_KV_LORA + MLA_ROPE
    return {
        "x_prompt": normal((BATCH, SEQ, D), 1.0),
        "x_sample": normal((DEC_BATCH, DEC_SEQ, D), 1.0),
        "state_ret": normal((N_RET_LAYERS, DEC_BATCH, RET_HEADS, RET_DK, RET_DV), 0.5),
        "cache_k_c": normal((N_DSA_LAYERS, n_pool, PAGE_SIZE, DSA_KV_HEADS, DSA_HEAD_DIM), 1.0),
        "cache_v_c": normal((N_DSA_LAYERS, n_pool, PAGE_SIZE, DSA_KV_HEADS, DSA_HEAD_DIM), 1.0),
        "cache_kidx_c": normal((N_DSA_LAYERS, n_pool, PAGE_SIZE, IDX_DIM), 1.0),
        "cache_ckv_d": normal((N_MLA_LAYERS, n_pool, PAGE_SIZE, MLA_KV_LORA), 1.0),
        "cache_kpe_d": normal((N_MLA_LAYERS, n_pool, PAGE_SIZE, MLA_ROPE), 1.0),
        "page_table": page_table,
        "norm_mix": gain((DEPTH, D)),
        "norm_ffn": gain((DEPTH, D)),
        "ret_w_in": normal((N_RET_LAYERS, D, ret_in), D ** -0.5),
        "ret_gn": gain((N_RET_LAYERS, RET_HEADS, RET_DV)),
        "ret_w_o": normal((N_RET_LAYERS, RET_HEADS * RET_DV, D), (RET_HEADS * RET_DV) ** -0.5),
        "cm_w_in": normal((N_CM_LAYERS, D, 2 * CM_WIDTH), D ** -0.5),
        "cm_ln_g": gain((N_CM_LAYERS, CM_WIDTH)),
        "cm_ln_b": normal((N_CM_LAYERS, CM_WIDTH), 0.02),
        "cm_w_s": normal((N_CM_LAYERS, CM_GROUPS, CM_CHUNK, CM_CHUNK), CM_CHUNK ** -0.5),
        "cm_b_s": gain((N_CM_LAYERS, CM_GROUPS, CM_CHUNK)),
        "cm_w_o": normal((N_CM_LAYERS, CM_WIDTH, D), CM_WIDTH ** -0.5),
        "dsa_w_in": normal((N_DSA_LAYERS, D, dsa_in), D ** -0.5),
        "dsa_q_norm": gain((N_DSA_LAYERS, DSA_HEAD_DIM)),
        "dsa_k_norm": gain((N_DSA_LAYERS, DSA_HEAD_DIM)),
        "dsa_w_o": normal((N_DSA_LAYERS, DSA_HEADS * DSA_HEAD_DIM, D), (DSA_HEADS * DSA_HEAD_DIM) ** -0.5),
        "mla_w_in": normal((N_MLA_LAYERS, D, mla_in), D ** -0.5),
        "mla_qa_norm": gain((N_MLA_LAYERS, MLA_Q_LORA)),
        "mla_kv_norm": gain((N_MLA_LAYERS, MLA_KV_LORA)),
        "mla_w_uq": normal((N_MLA_LAYERS, MLA_Q_LORA, MLA_HEADS * (MLA_NOPE + MLA_ROPE)), MLA_Q_LORA ** -0.5),
        "mla_qn_nope": gain((N_MLA_LAYERS, MLA_NOPE)),
        "mla_qn_pe": gain((N_MLA_LAYERS, MLA_ROPE)),
        "mla_kpe_norm": gain((N_MLA_LAYERS, MLA_ROPE)),
        "mla_w_uk": normal((N_MLA_LAYERS, MLA_KV_LORA, MLA_HEADS, MLA_NOPE), MLA_KV_LORA ** -0.5),
        "mla_w_uv": normal((N_MLA_LAYERS, MLA_KV_LORA, MLA_HEADS, MLA_VDIM), MLA_KV_LORA ** -0.5),
        "mla_w_o": normal((N_MLA_LAYERS, MLA_HEADS * MLA_VDIM, D), (MLA_HEADS * MLA_VDIM) ** -0.5),
        "moe_w_group": normal((DEPTH, D, MOE_GROUPS), D ** -0.5),
        "moe_b_group": normal((DEPTH, MOE_GROUPS), 0.01),
        "moe_w_expert": normal((DEPTH, D, MOE_EXPERTS), D ** -0.5),
        "moe_b_expert": normal((DEPTH, MOE_EXPERTS), 0.01),
        "moe_w1": normal((DEPTH, MOE_EXPERTS, D, MOE_FF), D ** -0.5),
        "moe_w3": normal((DEPTH, MOE_EXPERTS, D, MOE_FF), D ** -0.5),
        "moe_w2": normal((DEPTH, MOE_EXPERTS, MOE_FF, D), MOE_FF ** -0.5),
    }


def reference(x_prompt, x_sample, state_ret, cache_k_c, cache_v_c, cache_kidx_c, cache_ckv_d, cache_kpe_d,
              page_table, norm_mix, norm_ffn, ret_w_in, ret_gn, ret_w_o,
              cm_w_in, cm_ln_g, cm_ln_b, cm_w_s, cm_b_s, cm_w_o,
              dsa_w_in, dsa_q_norm, dsa_k_norm, dsa_w_o,
              mla_w_in, mla_qa_norm, mla_kv_norm, mla_w_uq, mla_qn_nope, mla_qn_pe, mla_kpe_norm,
              mla_w_uk, mla_w_uv, mla_w_o,
              moe_w_group, moe_b_group, moe_w_expert, moe_b_expert, moe_w1, moe_w3, moe_w2):
    hp, hs = x_prompt, x_sample
    past = page_table.shape[1] * PAGE_SIZE
    pos_p = jnp.arange(hp.shape[1])
    pos_s = past + jnp.arange(hs.shape[1])
    ret_p, ret_s, cm_s = [], [], []
    kc_p, vc_p, ic_p, kc_s, vc_s, ic_s = [], [], [], [], [], []
    ckv_p, kpe_p, ckv_s, kpe_s = [], [], [], []
    for i in range(DEPTH):
        kind, j = i % N_MIXERS, i // N_MIXERS
        xp, xs = rms_norm(hp, norm_mix[i]), rms_norm(hs, norm_mix[i])
        if kind == 0:
            zero_state = jnp.zeros((hp.shape[0], RET_HEADS, RET_DK, RET_DV), state_ret.dtype)
            op, st_p = retention_mixer(xp, pos_p, zero_state, ret_w_in[j], ret_gn[j], ret_w_o[j])
            os_, st_s = retention_mixer(xs, pos_s, state_ret[j], ret_w_in[j], ret_gn[j], ret_w_o[j])
            ret_p.append(st_p)
            ret_s.append(st_s)
        elif kind == 1:
            op, _ = chunk_mlp_mixer(xp, cm_w_in[j], cm_ln_g[j], cm_ln_b[j], cm_w_s[j], cm_b_s[j], cm_w_o[j])
            os_, v_rows = chunk_mlp_mixer(xs, cm_w_in[j], cm_ln_g[j], cm_ln_b[j], cm_w_s[j], cm_b_s[j], cm_w_o[j])
            cm_s.append(v_rows)
        elif kind == 2:
            op, k_new, v_new, i_new = dsa_prompt(xp, dsa_w_in[j], dsa_q_norm[j], dsa_k_norm[j], dsa_w_o[j])
            kc_p.append(k_new)
            vc_p.append(v_new)
            ic_p.append(i_new)
            os_, k_new, v_new, i_new = dsa_sample(xs, cache_k_c[j], cache_v_c[j], cache_kidx_c[j], page_table,
                                                  dsa_w_in[j], dsa_q_norm[j], dsa_k_norm[j], dsa_w_o[j])
            kc_s.append(k_new)
            vc_s.append(v_new)
            ic_s.append(i_new)
        else:
            op, c_new, r_new = mla_prompt(xp, mla_w_in[j], mla_qa_norm[j], mla_kv_norm[j], mla_w_uq[j],
                                          mla_qn_nope[j], mla_qn_pe[j], mla_kpe_norm[j],
                                          mla_w_uk[j], mla_w_uv[j], mla_w_o[j])
            ckv_p.append(c_new)
            kpe_p.append(r_new)
            os_, c_new, r_new = mla_sample(xs, cache_ckv_d[j], cache_kpe_d[j], page_table, mla_w_in[j],
                                           mla_qa_norm[j], mla_kv_norm[j], mla_w_uq[j], mla_qn_nope[j],
                                           mla_qn_pe[j], mla_kpe_norm[j], mla_w_uk[j], mla_w_uv[j], mla_w_o[j])
            ckv_s.append(c_new)
            kpe_s.append(r_new)
        hp = hp + op
        hs = hs + os_
        hp = hp + hier_moe(rms_norm(hp, norm_ffn[i]), moe_w_group[i], moe_b_group[i], moe_w_expert[i],
                           moe_b_expert[i], moe_w1[i], moe_w3[i], moe_w2[i])
        hs = hs + hier_moe(rms_norm(hs, norm_ffn[i]), moe_w_group[i], moe_b_group[i], moe_w_expert[i],
                           moe_b_expert[i], moe_w1[i], moe_w3[i], moe_w2[i])
    return (hp, hs,
            jnp.stack(ret_p), jnp.stack(ret_s), jnp.stack(cm_s),
            jnp.stack(kc_p), jnp.stack(vc_p), jnp.stack(ic_p),
            jnp.stack(kc_s), jnp.stack(vc_s), jnp.stack(ic_s),
            jnp.stack(ckv_p), jnp.stack(kpe_p), jnp.stack(ckv_s), jnp.stack(kpe_s))
```

```python
import functools
import math

import jax
import jax.numpy as jnp
from jax import lax
from jax.experimental import pallas as pl
from jax.experimental.pallas import tpu as pltpu

F32 = jnp.float32
BF16 = jnp.bfloat16
EPS = 1e-6
LANES = 128
VMEM_LIMIT = 56 * 1024 * 1024
NEG = -1e30

Q_BLOCK = 128
IDX_TOPK = 256
RET_THETA = 10000.0
DSA_THETA = 500000.0
MLA_THETA = 10000.0
MOE_TOPK = 2
MOE_ROWS = 128


def _cp(*sem):
    return pltpu.CompilerParams(dimension_semantics=sem, vmem_limit_bytes=VMEM_LIMIT)


def _tile(n, pref, mult=8):
    best = None
    for d in range(mult, min(n, pref) + 1, mult):
        if n % d == 0:
            best = d
    assert best is not None, (n, pref, mult)
    return best


def _rmsnorm_kernel(x_ref, g_ref, o_ref):
    x = x_ref[...]
    y = x * lax.rsqrt(jnp.mean(x * x, -1, keepdims=True) + EPS)
    o_ref[...] = (y * g_ref[...]).astype(o_ref.dtype)


def _rmsnorm(h, g, tm):
    n, d = h.shape
    return pl.pallas_call(
        _rmsnorm_kernel,
        grid=(n // tm,),
        in_specs=[pl.BlockSpec((tm, d), lambda i: (i, 0)), pl.BlockSpec((1, d), lambda i: (0, 0))],
        out_specs=pl.BlockSpec((tm, d), lambda i: (i, 0)),
        out_shape=jax.ShapeDtypeStruct((n, d), BF16),
        compiler_params=_cp("parallel"),
        name="rmsnorm",
    )(h, g.reshape(1, d))


def _mm_kernel(*refs, act, has_res):
    if has_res:
        a_ref, w_ref, r_ref, o_ref, wb_ref = refs
    else:
        a_ref, w_ref, o_ref, wb_ref = refs

    @pl.when(pl.program_id(1) == 0)
    def _():
        wb_ref[...] = w_ref[...].astype(BF16)

    acc = jnp.dot(a_ref[...].astype(BF16), wb_ref[...], preferred_element_type=F32)
    if act == "gelu":
        acc = jax.nn.gelu(acc)
    if has_res:
        acc = acc + r_ref[...]
    o_ref[...] = acc.astype(o_ref.dtype)


def _matmul(a, w, *, tm, out_dtype=F32, act=None, residual=None, name="matmul"):
    m, k = a.shape
    n = w.shape[1]
    tn = n if n <= 512 else (512 if k <= 2048 else 256)
    in_specs = [pl.BlockSpec((tm, k), lambda j, i: (i, 0)), pl.BlockSpec((k, tn), lambda j, i: (0, j))]
    args = [a, w]
    if residual is not None:
        in_specs.append(pl.BlockSpec((tm, tn), lambda j, i: (i, j)))
        args.append(residual)
    return pl.pallas_call(
        functools.partial(_mm_kernel, act=act, has_res=residual is not None),
        grid=(pl.cdiv(n, tn), m // tm),
        in_specs=in_specs,
        out_specs=pl.BlockSpec((tm, tn), lambda j, i: (i, j)),
        out_shape=jax.ShapeDtypeStruct((m, n), out_dtype),
        scratch_shapes=[pltpu.VMEM((k, tn), BF16)],
        compiler_params=_cp("parallel", "arbitrary"),
        name=name,
    )(*args)


def _head_mm_kernel(a_ref, w_ref, o_ref, wb_ref, *, trans_w):
    @pl.when(pl.program_id(1) == 0)
    def _():
        wb_ref[...] = w_ref[...].astype(BF16)

    a = a_ref[...].astype(BF16)
    if trans_w:
        acc = lax.dot_general(a, wb_ref[...], (((1,), (1,)), ((), ())), preferred_element_type=F32)
    else:
        acc = jnp.dot(a, wb_ref[...], preferred_element_type=F32)
    o_ref[...] = acc.astype(o_ref.dtype)


def _head_mm(a, w, heads, *, trans_w, tm, out_dtype, name):
    m = a.shape[0]
    ka = a.shape[1] // heads
    if trans_w:
        nw = w.shape[0]
        wblk = (nw, ka)
    else:
        nw = w.shape[1] // heads
        wblk = (ka, nw)
    return pl.pallas_call(
        functools.partial(_head_mm_kernel, trans_w=trans_w),
        grid=(heads, m // tm),
        in_specs=[pl.BlockSpec((tm, ka), lambda h, i: (i, h)), pl.BlockSpec(wblk, lambda h, i: (0, h))],
        out_specs=pl.BlockSpec((tm, nw), lambda h, i: (i, h)),
        out_shape=jax.ShapeDtypeStruct((m, heads * nw), out_dtype),
        scratch_shapes=[pltpu.VMEM(wblk, BF16)],
        compiler_params=_cp("parallel", "arbitrary"),
        name=name,
    )(a, w)


def _rope_tables_full(pos, theta, n_rot):
    half = n_rot // 2
    inv = theta ** (-jnp.arange(half, dtype=F32) / half)
    ang = pos.astype(F32)[:, None] * inv[None, :]
    return jnp.cos(ang), jnp.sin(ang)


def _rope_tables_packed(pos, theta, n_rot, group):
    half = n_rot // 2
    cos, sin = _rope_tables_full(pos, theta, n_rot)
    lane = jnp.arange(LANES)
    within = lane % group
    fidx = within % half
    c = jnp.where(within < n_rot, cos[:, fidx], 1.0)
    s1 = jnp.where(within < half, -sin[:, fidx], 0.0)
    s2 = jnp.where((within >= half) & (within < n_rot), sin[:, fidx], 0.0)
    return c.astype(F32), s1.astype(F32), s2.astype(F32)


def _rope_packed(y, c, s1, s2, half):
    return y * c + pltpu.roll(y, LANES - half, 1) * s1 + pltpu.roll(y, half, 1) * s2


def _rope_split(x, cos, sin):
    half = cos.shape[-1]
    x1, x2 = x[:, :half], x[:, half:]
    return jnp.concatenate([x1 * cos - x2 * sin, x2 * cos + x1 * sin], -1)


def _groupnorm_gate(o, gn, g):
    mu = jnp.mean(o, -1, keepdims=True)
    d = o - mu
    var = jnp.mean(d * d, -1, keepdims=True)
    return (d * lax.rsqrt(var + EPS) * gn) * jax.nn.silu(g)


def _ret_prompt_kernel(q_ref, k_ref, v_ref, g_ref, cos_ref, sin_ref, din_ref, dq_ref, dk_ref, dch_ref,
                       gn_ref, o_ref, s_ref, *, scale):
    @pl.when(pl.program_id(2) == 0)
    def _():
        s_ref[...] = jnp.zeros_like(s_ref)

    cos, sin = cos_ref[...], sin_ref[...]
    q = _rope_split(q_ref[...], cos, sin)
    k = _rope_split(k_ref[...], cos, sin) * scale
    qb, kb, vb = q.astype(BF16), k.astype(BF16), v_ref[...].astype(BF16)
    state = s_ref[0, 0]
    att = lax.dot_general(qb, kb, (((1,), (1,)), ((), ())), preferred_element_type=F32) * din_ref[0]
    o = jnp.dot(att.astype(BF16), vb, preferred_element_type=F32)
    o = o + jnp.dot(qb, state.astype(BF16), preferred_element_type=F32) * dq_ref[0]
    kdt = (k * dk_ref[0]).T.astype(BF16)
    s_ref[0, 0] = state * dch_ref[0] + jnp.dot(kdt, vb, preferred_element_type=F32)
    o_ref[...] = _groupnorm_gate(o, gn_ref[0], g_ref[...]).astype(o_ref.dtype)


def _ret_sample_kernel(q_ref, k_ref, v_ref, g_ref, cos_ref, sin_ref, din_ref, dq_ref, dk_ref, dch_ref,
                       gn_ref, s0_ref, o_ref, s_ref, *, scale, nb, t):
    rows = nb * t
    cos, sin = cos_ref[...], sin_ref[...]
    q = _rope_split(q_ref[...], cos, sin)
    k = _rope_split(k_ref[...], cos, sin) * scale
    qb, kb, vb = q.astype(BF16), k.astype(BF16), v_ref[...].astype(BF16)
    att = lax.dot_general(qb, kb, (((1,), (1,)), ((), ())), preferred_element_type=F32) * din_ref[0]
    o = jnp.dot(att.astype(BF16), vb, preferred_element_type=F32)
    kd = k * dk_ref[0]
    pad = LANES - rows
    kdt = jnp.concatenate([kd, jnp.zeros((pad, kd.shape[1]), F32)], 0).T
    vpad = jnp.concatenate([vb, jnp.zeros((pad, vb.shape[1]), BF16)], 0)
    rowb = lax.broadcasted_iota(jnp.int32, (rows, 1), 0) // t
    colb = lax.broadcasted_iota(jnp.int32, (1, LANES), 1) // t
    dq = dq_ref[0]
    dch = dch_ref[0]
    for i in range(nb):
        state = s0_ref[i, 0]
        cross = jnp.dot(qb, state.astype(BF16), preferred_element_type=F32) * dq
        o = o + jnp.where(rowb == i, cross, 0.0)
        kdt_i = jnp.where(colb == i, kdt, 0.0).astype(BF16)
        s_ref[i, 0] = state * dch + jnp.dot(kdt_i, vpad, preferred_element_type=F32)
    o_ref[...] = _groupnorm_gate(o, gn_ref[0], g_ref[...]).astype(o_ref.dtype)


def _retention(proj, state_s, gn, dims):
    B, L, Bs, T, past = dims
    Np, Ns = B * L, Bs * T
    H, DK, DV = state_s.shape[1:]
    half = DK // 2
    scale = DK ** -0.5
    vblk0 = (2 * H * DK) // DV
    log_gamma = jnp.log1p(-(2.0 ** (-5.0 - jnp.arange(H, dtype=F32))))

    def decay(C):
        idx = jnp.arange(C, dtype=F32)
        rel = idx[:, None] - idx[None, :]
        d_inner = jnp.where(rel >= 0, jnp.exp(log_gamma[:, None, None] * jnp.maximum(rel, 0.0)), 0.0)
        d_query = jnp.exp((idx[None, :] + 1.0) * log_gamma[:, None])[:, :, None]
        d_key = jnp.exp((C - 1.0 - idx[None, :]) * log_gamma[:, None])[:, :, None]
        d_chunk = jnp.broadcast_to(jnp.exp(C * log_gamma)[:, None, None], (H, 1, DV))
        return d_inner, d_query, d_key, d_chunk

    gn3 = gn.reshape(H, 1, DV)
    C = 128 if L % 128 == 0 else L
    n = L // C
    cos, sin = _rope_tables_full(jnp.arange(L), RET_THETA, DK)
    d_inner, d_query, d_key, d_chunk = decay(C)
    row = lambda b, h, c: b * n + c
    o_p, s_p = pl.pallas_call(
        functools.partial(_ret_prompt_kernel, scale=scale),
        grid=(B, H, n),
        in_specs=[
            pl.BlockSpec((C, DK), lambda b, h, c: (row(b, h, c), h)),
            pl.BlockSpec((C, DK), lambda b, h, c: (row(b, h, c), H + h)),
            pl.BlockSpec((C, DV), lambda b, h, c: (row(b, h, c), vblk0 + h)),
            pl.BlockSpec((C, DV), lambda b, h, c: (row(b, h, c), vblk0 + H + h)),
            pl.BlockSpec((C, half), lambda b, h, c: (c, 0)),
            pl.BlockSpec((C, half), lambda b, h, c: (c, 0)),
            pl.BlockSpec((1, C, C), lambda b, h, c: (h, 0, 0)),
            pl.BlockSpec((1, C, 1), lambda b, h, c: (h, 0, 0)),
            pl.BlockSpec((1, C, 1), lambda b, h, c: (h, 0, 0)),
            pl.BlockSpec((1, 1, DV), lambda b, h, c: (h, 0, 0)),
            pl.BlockSpec((1, 1, DV), lambda b, h, c: (h, 0, 0)),
        ],
        out_specs=[
            pl.BlockSpec((C, DV), lambda b, h, c: (row(b, h, c), h)),
            pl.BlockSpec((1, 1, DK, DV), lambda b, h, c: (b, h, 0, 0)),
        ],
        out_shape=[jax.ShapeDtypeStruct((Np, H * DV), BF16), jax.ShapeDtypeStruct((B, H, DK, DV), F32)],
        compiler_params=_cp("parallel", "parallel", "arbitrary"),
        name="retention_prompt",
    )(proj, proj, proj, proj, cos, sin, d_inner, d_query, d_key, d_chunk, gn3)

    nb = _tile(Bs, 8, 1)
    while (nb * T) % 8 or Np % (nb * T):
        nb -= 1
    rows = nb * T
    cos_s, sin_s = _rope_tables_full(past + jnp.arange(T), RET_THETA, DK)
    cos_s, sin_s = jnp.tile(cos_s, (nb, 1)), jnp.tile(sin_s, (nb, 1))
    di, dqs, dks, dchs = decay(T)
    same = (jnp.arange(rows)[:, None] // T) == (jnp.arange(rows)[None, :] // T)
    di = jnp.where(same[None], jnp.tile(di, (1, nb, nb)), 0.0)
    dqs, dks = jnp.tile(dqs, (1, nb, 1)), jnp.tile(dks, (1, nb, 1))
    r0 = Np // rows
    o_s, s_s = pl.pallas_call(
        functools.partial(_ret_sample_kernel, scale=scale, nb=nb, t=T),
        grid=(Bs // nb, H),
        in_specs=[
            pl.BlockSpec((rows, DK), lambda b, h: (r0 + b, h)),
            pl.BlockSpec((rows, DK), lambda b, h: (r0 + b, H + h)),
            pl.BlockSpec((rows, DV), lambda b, h: (r0 + b, vblk0 + h)),
            pl.BlockSpec((rows, DV), lambda b, h: (r0 + b, vblk0 + H + h)),
            pl.BlockSpec((rows, half), lambda b, h: (0, 0)),
            pl.BlockSpec((rows, half), lambda b, h: (0, 0)),
            pl.BlockSpec((1, rows, rows), lambda b, h: (h, 0, 0)),
            pl.BlockSpec((1, rows, 1), lambda b, h: (h, 0, 0)),
            pl.BlockSpec((1, rows, 1), lambda b, h: (h, 0, 0)),
            pl.BlockSpec((1, 1, DV), lambda b, h: (h, 0, 0)),
            pl.BlockSpec((1, 1, DV), lambda b, h: (h, 0, 0)),
            pl.BlockSpec((nb, 1, DK, DV), lambda b, h: (b, h, 0, 0)),
        ],
        out_specs=[
            pl.BlockSpec((rows, DV), lambda b, h: (b, h)),
            pl.BlockSpec((nb, 1, DK, DV), lambda b, h: (b, h, 0, 0)),
        ],
        out_shape=[jax.ShapeDtypeStruct((Ns, H * DV), BF16), jax.ShapeDtypeStruct((Bs, H, DK, DV), F32)],
        compiler_params=_cp("parallel", "parallel"),
        name="retention_sample",
    )(proj, proj, proj, proj, cos_s, sin_s, di, dqs, dks, dchs, gn3, state_s)
    return jnp.concatenate([o_p, o_s], 0), s_p, s_s


def _cm_gate_kernel(u_ref, v_ref, lg_ref, lb_ref, ws_ref, bs_ref, y_ref, vn_ref, *, groups):
    v = v_ref[...]
    mu = jnp.mean(v, -1, keepdims=True)
    d = v - mu
    var = jnp.mean(d * d, -1, keepdims=True)
    vn = d * lax.rsqrt(var + EPS) * lg_ref[...] + lb_ref[...]
    vn_ref[...] = vn
    cw = v.shape[1] // groups
    for g in range(groups):
        sl = slice(g * cw, (g + 1) * cw)
        s = jnp.dot(ws_ref[0, g].astype(BF16), vn[:, sl].astype(BF16), preferred_element_type=F32) + bs_ref[0, g]
        y_ref[:, sl] = (u_ref[:, sl] * s).astype(y_ref.dtype)


def _cm_gate(uv, ln_g, ln_b, w_s, b_s, dims):
    B, L, Bs, T, past = dims
    Np, Ns = B * L, Bs * T
    W = uv.shape[1] // 2
    G, C = w_s.shape[0], w_s.shape[1]
    assert L % C == 0 and Ns % C == 0 and C % T == 0 and T <= C
    npc = Np // C
    r = jnp.arange(C)
    tril = r[:, None] >= r[None, :]
    ws_p = jnp.where(tril[None], w_s, 0.0)
    tt = r % T
    same = (r[:, None] // T) == (r[None, :] // T)
    ws_s = jnp.where((same & (tt[:, None] >= tt[None, :]))[None], w_s[:, tt[:, None], tt[None, :]], 0.0)
    ws2 = jnp.stack([ws_p, ws_s])
    bs2 = jnp.stack([b_s, b_s[:, tt]])[..., None]
    kind = lambda i: jnp.where(i < npc, 0, 1)
    y, vn = pl.pallas_call(
        functools.partial(_cm_gate_kernel, groups=G),
        grid=((Np + Ns) // C,),
        in_specs=[
            pl.BlockSpec((C, W), lambda i: (i, 0)),
            pl.BlockSpec((C, W), lambda i: (i, 1)),
            pl.BlockSpec((1, W), lambda i: (0, 0)),
            pl.BlockSpec((1, W), lambda i: (0, 0)),
            pl.BlockSpec((1, G, C, C), lambda i: (kind(i), 0, 0, 0)),
            pl.BlockSpec((1, G, C, 1), lambda i: (kind(i), 0, 0, 0)),
        ],
        out_specs=[
            pl.BlockSpec((C, W), lambda i: (i, 0)),
            pl.BlockSpec((C, W), lambda i: (jnp.maximum(i - npc, 0), 0)),
        ],
        out_shape=[jax.ShapeDtypeStruct((Np + Ns, W), BF16), jax.ShapeDtypeStruct((Ns, W), F32)],
        compiler_params=_cp("arbitrary"),
        name="cm_gate",
    )(uv, uv, ln_g.reshape(1, W), ln_b.reshape(1, W), ws2, bs2)
    return y, vn


def _topk_mask(sc, valid, k, idx_bits):
    int_min = jnp.int32(-2 ** 31)
    kf = jnp.float32(k)
    bits = pltpu.bitcast(sc + 0.0, jnp.int32)
    key = jnp.where(bits < 0, bits ^ jnp.int32(0x7FFFFFFF), bits)
    key = jnp.where(valid, key, int_min)

    def count(m):
        return jnp.sum(m.astype(F32), axis=1, keepdims=True)

    t0 = jnp.where(count(key >= 0) >= kf, jnp.int32(0), int_min)

    def value_bit(i, t):
        cand = t | (jnp.int32(1) << (30 - i))
        return jnp.where(count(key >= cand) >= kf, cand, t)

    thr = lax.fori_loop(0, 31, value_bit, t0)
    gt = key > thr
    eq = key == thr
    need = kf - count(gt)
    idx = lax.broadcasted_iota(jnp.int32, sc.shape, 1)

    def index_bit(i, x):
        cand = x | (jnp.int32(1) << (idx_bits - 1 - i))
        return jnp.where(count(eq & (idx < cand)) < need, cand, x)

    last = lax.fori_loop(0, idx_bits, index_bit, jnp.zeros_like(thr))
    return valid & (gt | (eq & (idx <= last)))


def _dsa_prep_kernel(x_ref, qn_ref, kn_ref, c_ref, s1_ref, s2_ref, ci_ref, si1_ref, si2_ref,
                     q_ref, k_ref, kb_ref, v_ref, vb_ref, qi_ref, ki_ref, kib_ref, wh_ref,
                     *, heads, kv_heads, idx_heads, half, half_i, wh_scale):
    hd = LANES
    c, s1, s2 = c_ref[...], s1_ref[...], s2_ref[...]
    ci, si1, si2 = ci_ref[...], si1_ref[...], si2_ref[...]

    def normed(x, gain):
        return x * lax.rsqrt(jnp.mean(x * x, -1, keepdims=True) + EPS) * gain

    off = 0
    for h in range(heads):
        y = _rope_packed(normed(x_ref[:, off:off + hd], qn_ref[...]), c, s1, s2, half)
        q_ref[:, h * hd:(h + 1) * hd] = y.astype(q_ref.dtype)
        off += hd
    for g in range(kv_heads):
        y = _rope_packed(normed(x_ref[:, off:off + hd], kn_ref[...]), c, s1, s2, half)
        k_ref[:, g * hd:(g + 1) * hd] = y
        kb_ref[:, g * hd:(g + 1) * hd] = y.astype(BF16)
        off += hd
    v = x_ref[:, off:off + kv_heads * hd]
    v_ref[...] = v
    vb_ref[...] = v.astype(BF16)
    off += kv_heads * hd
    for h in range(idx_heads):
        y = _rope_packed(x_ref[:, off:off + hd], ci, si1, si2, half_i)
        qi_ref[:, h * hd:(h + 1) * hd] = y.astype(qi_ref.dtype)
        off += hd
    y = _rope_packed(x_ref[:, off:off + hd], ci, si1, si2, half_i)
    ki_ref[...] = y
    kib_ref[...] = y.astype(BF16)
    off += hd
    wh_ref[...] = x_ref[:, off:off + idx_heads] * wh_scale


def _dsa_prompt_kernel(q_ref, qi_ref, wh_ref, k_ref, v_ref, ki_ref, o_ref, bias_ref,
                       *, kv_heads, rep, idx_heads, k_top, idx_bits, idx_scale, scale):
    hd = LANES
    tq, L = bias_ref.shape
    kib = ki_ref[...]
    wh = wh_ref[...] * idx_scale
    sc = jnp.zeros((tq, L), F32)
    for h in range(idx_heads):
        d = lax.dot_general(qi_ref[:, h * hd:(h + 1) * hd], kib, (((1,), (1,)), ((), ())),
                            preferred_element_type=F32)
        sc = sc + wh[:, h:h + 1] * jnp.maximum(d, 0.0)
    qpos = pl.program_id(1) * tq + lax.broadcasted_iota(jnp.int32, (tq, L), 0)
    spos = lax.broadcasted_iota(jnp.int32, (tq, L), 1)
    mask = _topk_mask(sc, spos <= qpos, k_top, idx_bits)
    bias_ref[...] = jnp.where(mask, 0.0, -jnp.inf)
    for g in range(kv_heads):
        kg = k_ref[:, g * hd:(g + 1) * hd]
        vg = v_ref[:, g * hd:(g + 1) * hd]
        for r in range(rep):
            hq = g * rep + r
            s = lax.dot_general(q_ref[:, hq * hd:(hq + 1) * hd], kg, (((1,), (1,)), ((), ())),
                                preferred_element_type=F32) * scale + bias_ref[...]
            e = jnp.exp(s - jnp.max(s, -1, keepdims=True))
            p = (e * (1.0 / jnp.sum(e, -1, keepdims=True))).astype(BF16)
            o_ref[:, hq * hd:(hq + 1) * hd] = jnp.dot(p, vg, preferred_element_type=F32).astype(o_ref.dtype)


def _paged_kernel(body, n_paged, n_dense, n_out, pc, n_pages, page_rows):
    nchunks = n_pages // pc

    def kernel(pt_ref, *refs):
        paged = refs[:n_paged]
        dense = refs[n_paged:n_paged + n_dense]
        outs = refs[n_paged + n_dense:n_paged + n_dense + n_out]
        rest = refs[n_paged + n_dense + n_out:]
        bufs, sem, user = rest[:n_paged], rest[n_paged], rest[n_paged + 1:]
        c = pl.program_id(1)
        step = pl.program_id(0) * nchunks + c
        total = pl.num_programs(0) * nchunks
        slot = step % 2

        def copies(st, sl):
            base = (st // nchunks) * n_pages + (st % nchunks) * pc
            return [pltpu.make_async_copy(paged[a].at[pt_ref[base + p]],
                                          bufs[a].at[sl, pl.ds(p * page_rows, page_rows)],
                                          sem.at[sl, a])
                    for a in range(n_paged) for p in range(pc)]

        @pl.when(step == 0)
        def _():
            for cp in copies(step, slot):
                cp.start()

        @pl.when(step + 1 < total)
        def _():
            for cp in copies(step + 1, 1 - slot):
                cp.start()

        for cp in copies(step, slot):
            cp.wait()
        body(c, nchunks, [bufs[a].at[slot] for a in range(n_paged)], dense, outs, user)

    return kernel


def _paged_call(body, page_table, paged, dense, dense_specs, out_shapes, out_specs, user_scratch, pc, name):
    Bs, n_pages = page_table.shape
    page_rows = paged[0].shape[1]
    nchunks = n_pages // pc
    kernel = _paged_kernel(body, len(paged), len(dense), 1, pc, n_pages, page_rows)
    scratch = [pltpu.VMEM((2, pc * page_rows, a.shape[2]), a.dtype) for a in paged]
    scratch.append(pltpu.SemaphoreType.DMA((2, len(paged))))
    scratch.extend(user_scratch)
    grid_spec = pltpu.PrefetchScalarGridSpec(
        num_scalar_prefetch=1,
        grid=(Bs, nchunks),
        in_specs=[pl.BlockSpec(memory_space=pl.ANY)] * len(paged) + list(dense_specs),
        out_specs=out_specs,
        scratch_shapes=scratch,
    )
    return pl.pallas_call(
        kernel, grid_spec=grid_spec, out_shape=out_shapes,
        compiler_params=_cp("arbitrary", "arbitrary"), name=name,
    )(page_table.reshape(-1), *paged, *dense)


def _dsa_sample_select_body(c, nchunks, bufs, dense, outs, user, *, t, idx_heads, k_top, idx_bits, idx_scale):
    (ki_buf,) = bufs
    qi_ref, wh_ref, kin_ref = dense
    (bias_ref,) = outs
    (sc_ref,) = user
    w = ki_buf.shape[0]
    past = nchunks * w

    def scores(keys_bf16):
        d = lax.dot_general(qi_ref[0], keys_bf16, (((1,), (1,)), ((), ())), preferred_element_type=F32)
        wd = jnp.maximum(d, 0.0) * (wh_ref[0] * idx_scale)
        return jnp.sum(wd.reshape(t, idx_heads, wd.shape[1]), axis=1)

    @pl.when(c == 0)
    def _():
        sc_ref[...] = jnp.zeros_like(sc_ref)

    sc_ref[0:t, pl.ds(pl.multiple_of(c * w, LANES), w)] = scores(ki_buf[...].astype(BF16))

    @pl.when(c == nchunks - 1)
    def _():
        sc_ref[0:t, past:past + LANES] = scores(kin_ref[0])
        rows, S = sc_ref.shape
        row = lax.broadcasted_iota(jnp.int32, (rows, S), 0)
        col = lax.broadcasted_iota(jnp.int32, (rows, S), 1)
        valid = (col < past) | ((col - past <= row) & (col - past < t))
        mask = _topk_mask(sc_ref[...], valid, k_top, idx_bits)
        bias_ref[0] = jnp.where(mask, 0.0, NEG)[0:t]


def _expand_rows(x, reps):
    t, w = x.shape
    row = lax.broadcasted_iota(jnp.int32, (t * reps, w), 0) // reps
    out = jnp.broadcast_to(x[0:1], (t * reps, w))
    for i in range(1, t):
        out = jnp.where(row == i, jnp.broadcast_to(x[i:i + 1], (t * reps, w)), out)
    return out


def _softmax_update(s, v_bf16, m_ref, l_ref, acc_ref, idx):
    m_old = m_ref[idx]
    m_new = jnp.maximum(m_old, jnp.max(s, -1, keepdims=True))
    a = jnp.exp(m_old - m_new)
    p = jnp.exp(s - m_new)
    l_ref[idx] = l_ref[idx] * a + jnp.sum(p, -1, keepdims=True)
    acc_ref[idx] = acc_ref[idx] * a + jnp.dot(p.astype(BF16), v_bf16, preferred_element_type=F32)
    m_ref[idx] = m_new


def _dsa_sample_attend_body(c, nchunks, bufs, dense, outs, user, *, kv_heads, rep, scale):
    k_buf, v_buf = bufs
    q_ref, bias_ref, tail_ref, kn_ref, vn_ref = dense
    (o_ref,) = outs
    m_ref, l_ref, acc_ref = user
    hd = LANES

    @pl.when(c == 0)
    def _():
        m_ref[...] = jnp.full_like(m_ref, NEG)
        l_ref[...] = jnp.zeros_like(l_ref)
        acc_ref[...] = jnp.zeros_like(acc_ref)

    def attend(keys, vals, bias):
        bias = _expand_rows(bias, rep)
        for g in range(kv_heads):
            s = lax.dot_general(q_ref[0, g], keys[:, g * hd:(g + 1) * hd], (((1,), (1,)), ((), ())),
                                preferred_element_type=F32) * scale + bias
            _softmax_update(s, vals[:, g * hd:(g + 1) * hd], m_ref, l_ref, acc_ref, g)

    attend(k_buf[...].astype(BF16), v_buf[...].astype(BF16), bias_ref[0])

    @pl.when(c == nchunks - 1)
    def _():
        attend(kn_ref[0], vn_ref[0], tail_ref[0])
        o_ref[0] = (acc_ref[...] / l_ref[...]).astype(o_ref.dtype)


def _dsa(proj, caches, page_table, q_norm, k_norm, dims, dsa_heads):
    cache_k, cache_v, cache_ki = caches
    B, L, Bs, T, past = dims
    Np, Ns = B * L, Bs * T
    N = Np + Ns
    n_pool, page, KVH, HD = cache_k.shape
    IDX_DIM = cache_ki.shape[-1]
    H = dsa_heads
    REP = H // KVH
    IDXH = (proj.shape[1] - H * HD - 2 * KVH * HD - IDX_DIM) // (IDX_DIM + 1)
    assert HD == LANES and IDX_DIM == LANES
    rope_dims, idx_rope = HD // 4, IDX_DIM // 4
    idx_scale = IDX_DIM ** -0.5
    scale = HD ** -0.5

    pos = jnp.concatenate([jnp.arange(L), jnp.tile(past + jnp.arange(T), Bs)])
    tabs = _rope_tables_packed(pos, DSA_THETA, rope_dims, HD) + _rope_tables_packed(pos, DSA_THETA, idx_rope, IDX_DIM)
    tm = _tile(math.gcd(L, Ns), 256)
    lpt, npt = L // tm, Np // tm
    tab_spec = pl.BlockSpec((tm, LANES), lambda i: (jnp.where(i < npt, i % lpt, lpt + i - npt), 0))
    rowspec = lambda w: pl.BlockSpec((tm, w), lambda i: (i, 0))
    shapes = [(H * HD, BF16), (KVH * HD, F32), (KVH * HD, BF16), (KVH * HD, F32), (KVH * HD, BF16),
              (IDXH * IDX_DIM, BF16), (IDX_DIM, F32), (IDX_DIM, BF16), (IDXH, F32)]
    q, k, kb, v, vb, qi, ki, kib, wh = pl.pallas_call(
        functools.partial(_dsa_prep_kernel, heads=H, kv_heads=KVH, idx_heads=IDXH, half=rope_dims // 2,
                          half_i=idx_rope // 2, wh_scale=IDXH ** -0.5),
        grid=(N // tm,),
        in_specs=[rowspec(proj.shape[1]), pl.BlockSpec((1, HD), lambda i: (0, 0)),
                  pl.BlockSpec((1, HD), lambda i: (0, 0))] + [tab_spec] * 6,
        out_specs=[rowspec(w) for w, _ in shapes],
        out_shape=[jax.ShapeDtypeStruct((N, w), dt) for w, dt in shapes],
        compiler_params=_cp("parallel"),
        name="dsa_prep",
    )(proj, q_norm.reshape(1, HD), k_norm.reshape(1, HD), *tabs)

    tq = Q_BLOCK
    nq = L // tq
    k_top = min(IDX_TOPK, L // 4)
    o_p = pl.pallas_call(
        functools.partial(_dsa_prompt_kernel, kv_heads=KVH, rep=REP, idx_heads=IDXH, k_top=k_top,
                          idx_bits=max(1, (L - 1).bit_length()), idx_scale=idx_scale, scale=scale),
        grid=(B, nq),
        in_specs=[
            pl.BlockSpec((tq, H * HD), lambda b, i: (b * nq + i, 0)),
            pl.BlockSpec((tq, IDXH * IDX_DIM), lambda b, i: (b * nq + i, 0)),
            pl.BlockSpec((tq, IDXH), lambda b, i: (b * nq + i, 0)),
            pl.BlockSpec((L, KVH * HD), lambda b, i: (b, 0)),
            pl.BlockSpec((L, KVH * HD), lambda b, i: (b, 0)),
            pl.BlockSpec((L, IDX_DIM), lambda b, i: (b, 0)),
        ],
        out_specs=pl.BlockSpec((tq, H * HD), lambda b, i: (b * nq + i, 0)),
        out_shape=jax.ShapeDtypeStruct((Np, H * HD), BF16),
        scratch_shapes=[pltpu.VMEM((tq, L), F32)],
        compiler_params=_cp("parallel", "arbitrary"),
        name="dsa_prompt",
    )(q, qi, wh, kb, vb, kib)

    n_pages = page_table.shape[1]
    pc = _tile(n_pages, 8, 1)
    w = pc * page
    S = past + LANES
    k_top_s = min(IDX_TOPK, (past + T) // 4)
    rows = 8
    qi_s = qi[Np:].reshape(Bs, T * IDXH, IDX_DIM)
    wh_s = wh[Np:].reshape(Bs, T * IDXH, 1)
    pad_new = lambda x: jnp.pad(x[Np:].reshape(Bs, T, x.shape[1]), ((0, 0), (0, LANES - T), (0, 0)))
    bias = _paged_call(
        functools.partial(_dsa_sample_select_body, t=T, idx_heads=IDXH, k_top=k_top_s,
                          idx_bits=(S - 1).bit_length(), idx_scale=idx_scale),
        page_table, [cache_ki],
        [qi_s, wh_s, pad_new(kib)],
        [pl.BlockSpec((1, T * IDXH, IDX_DIM), lambda b, c, pt: (b, 0, 0)),
         pl.BlockSpec((1, T * IDXH, 1), lambda b, c, pt: (b, 0, 0)),
         pl.BlockSpec((1, LANES, IDX_DIM), lambda b, c, pt: (b, 0, 0))],
        jax.ShapeDtypeStruct((Bs, T, S), F32),
        pl.BlockSpec((1, T, S), lambda b, c, pt: (b, 0, 0)),
        [pltpu.VMEM((rows, S), F32)],
        pc, "dsa_sample_select")

    q_s = q[Np:].reshape(Bs, T, KVH, REP, HD).transpose(0, 2, 1, 3, 4).reshape(Bs, KVH, T * REP, HD)
    o_s = _paged_call(
        functools.partial(_dsa_sample_attend_body, kv_heads=KVH, rep=REP, scale=scale),
        page_table, [cache_k.reshape(n_pool, page, KVH * HD), cache_v.reshape(n_pool, page, KVH * HD)],
        [q_s, bias, bias, pad_new(kb), pad_new(vb)],
        [pl.BlockSpec((1, KVH, T * REP, HD), lambda b, c, pt: (b, 0, 0, 0)),
         pl.BlockSpec((1, T, w), lambda b, c, pt: (b, 0, c)),
         pl.BlockSpec((1, T, LANES), lambda b, c, pt: (b, 0, past // LANES)),
         pl.BlockSpec((1, LANES, KVH * HD), lambda b, c, pt: (b, 0, 0)),
         pl.BlockSpec((1, LANES, KVH * HD), lambda b, c, pt: (b, 0, 0))],
        jax.ShapeDtypeStruct((Bs, KVH, T * REP, HD), BF16),
        pl.BlockSpec((1, KVH, T * REP, HD), lambda b, c, pt: (b, 0, 0, 0)),
        [pltpu.VMEM((KVH, T * REP, 1), F32), pltpu.VMEM((KVH, T * REP, 1), F32),
         pltpu.VMEM((KVH, T * REP, HD), F32)],
        pc, "dsa_sample_attend")
    o_s = o_s.reshape(Bs, KVH, T, REP, HD).transpose(0, 2, 1, 3, 4).reshape(Ns, H * HD)
    return jnp.concatenate([o_p, o_s], 0), k, v, ki


def _norm_rope_pairs(x, gain, c, s1, s2, group, half):
    lane = lax.broadcasted_iota(jnp.int32, x.shape, 1)
    x2 = x * x
    inv = jnp.zeros_like(x)
    for j in range(LANES // group):
        sel = (lane >= j * group) & (lane < (j + 1) * group)
        ms = jnp.sum(jnp.where(sel, x2, 0.0), -1, keepdims=True) * (1.0 / group)
        inv = jnp.where(sel, lax.rsqrt(ms + EPS), inv)
    return _rope_packed(x * inv * gain, c, s1, s2, half)


def _mla_prep_kernel(x_ref, qa_ref, kvn_ref, kpn_ref, c_ref, s1_ref, s2_ref,
                     cq_ref, ckv_ref, ckvb_ref, kpe_ref, kpeb_ref, *, q_lora, kv_lora, rope):
    def normed(x, gain):
        return x * lax.rsqrt(jnp.mean(x * x, -1, keepdims=True) + EPS) * gain

    cq_ref[...] = normed(x_ref[:, 0:q_lora], qa_ref[...]).astype(cq_ref.dtype)
    ckv = normed(x_ref[:, q_lora:q_lora + kv_lora], kvn_ref[...])
    ckv_ref[...] = ckv
    ckvb_ref[...] = ckv.astype(BF16)
    y = _norm_rope_pairs(x_ref[:, q_lora + kv_lora:q_lora + kv_lora + LANES], kpn_ref[...],
                         c_ref[...], s1_ref[...], s2_ref[...], rope, rope // 2)
    kpe_ref[...] = y[:, 0:rope]
    kpeb_ref[...] = y[:, 0:rope].astype(BF16)


def _mla_q_kernel(x_ref, gn_ref, gp_ref, c_ref, s1_ref, s2_ref, qn_ref, qp_ref, *, heads, nope, rope):
    def normed(x, gain):
        return x * lax.rsqrt(jnp.mean(x * x, -1, keepdims=True) + EPS) * gain

    for h in range(heads):
        qn_ref[:, h * nope:(h + 1) * nope] = normed(x_ref[:, h * nope:(h + 1) * nope], gn_ref[...]).astype(qn_ref.dtype)
    base = heads * nope
    c, s1, s2 = c_ref[...], s1_ref[...], s2_ref[...]
    for j in range(heads * rope // LANES):
        y = _norm_rope_pairs(x_ref[:, base + j * LANES:base + (j + 1) * LANES], gp_ref[...], c, s1, s2, rope, rope // 2)
        qp_ref[:, j * LANES:(j + 1) * LANES] = y.astype(qp_ref.dtype)


def _mla_prompt_kernel(qn_ref, qp_ref, kn_ref, kp_ref, v_ref, o_ref, *, heads, nope, rope, vdim, scale):
    tq = qn_ref.shape[0]
    L = kn_ref.shape[0]
    qpos = pl.program_id(1) * tq + lax.broadcasted_iota(jnp.int32, (tq, L), 0)
    spos = lax.broadcasted_iota(jnp.int32, (tq, L), 1)
    causal = spos <= qpos
    kp = kp_ref[...]
    dn = (((1,), (1,)), ((), ()))
    per_tile = LANES // rope
    lane = lax.broadcasted_iota(jnp.int32, (tq, LANES), 1)
    for h in range(heads):
        j, sub = h // per_tile, h % per_tile
        qp = qp_ref[:, j * LANES:(j + 1) * LANES]
        qp = jnp.where((lane >= sub * rope) & (lane < (sub + 1) * rope), qp, jnp.zeros_like(qp))
        s = (lax.dot_general(qn_ref[:, h * nope:(h + 1) * nope], kn_ref[:, h * nope:(h + 1) * nope], dn,
                             preferred_element_type=F32)
             + lax.dot_general(qp, kp, dn, preferred_element_type=F32)) * scale
        s = jnp.where(causal, s, -jnp.inf)
        e = jnp.exp(s - jnp.max(s, -1, keepdims=True))
        p = (e * (1.0 / jnp.sum(e, -1, keepdims=True))).astype(BF16)
        o_ref[:, h * vdim:(h + 1) * vdim] = jnp.dot(p, v_ref[:, h * vdim:(h + 1) * vdim],
                                                    preferred_element_type=F32).astype(o_ref.dtype)


def _mla_sample_body(c, nchunks, bufs, dense, outs, user, *, t, heads, scale):
    c_buf, r_buf = bufs
    ql_ref, qp_ref, cn_ref, rn_ref = dense
    (o_ref,) = outs
    m_ref, l_ref, acc_ref = user
    dn = (((1,), (1,)), ((), ()))

    @pl.when(c == 0)
    def _():
        m_ref[...] = jnp.full_like(m_ref, NEG)
        l_ref[...] = jnp.zeros_like(l_ref)
        acc_ref[...] = jnp.zeros_like(acc_ref)

    def logits(lat, pe):
        return (lax.dot_general(ql_ref[0], lat, dn, preferred_element_type=F32)
                + lax.dot_general(qp_ref[0], pe, dn, preferred_element_type=F32)) * scale

    lat = c_buf[...].astype(BF16)
    _softmax_update(logits(lat, r_buf[...].astype(BF16)), lat, m_ref, l_ref, acc_ref, 0)

    @pl.when(c == nchunks - 1)
    def _():
        lat_n = cn_ref[0]
        s = logits(lat_n, rn_ref[0])
        trow = lax.broadcasted_iota(jnp.int32, s.shape, 0) // heads
        col = lax.broadcasted_iota(jnp.int32, s.shape, 1)
        s = jnp.where((col <= trow) & (col < t), s, NEG)
        _softmax_update(s, lat_n, m_ref, l_ref, acc_ref, 0)
        o_ref[0] = (acc_ref[0] / l_ref[0]).astype(o_ref.dtype)


def _mla(xn, caches, page_table, w_in, qa_norm, kv_norm, w_uq, qn_nope, qn_pe, kpe_norm, w_uk, w_uv, dims, tm):
    cache_ckv, cache_kpe = caches
    B, L, Bs, T, past = dims
    Np, Ns = B * L, Bs * T
    N = Np + Ns
    KV = cache_ckv.shape[-1]
    R = cache_kpe.shape[-1]
    QL = w_uq.shape[0]
    H, NOPE = w_uk.shape[1], w_uk.shape[2]
    VD = w_uv.shape[2]
    assert NOPE == LANES and LANES % R == 0 and (H * R) % LANES == 0 and (QL + KV) % LANES == 0
    scale = (NOPE + R) ** -0.5

    w_in_p = jnp.pad(w_in, ((0, 0), (0, LANES - R)))
    proj = _matmul(xn, w_in_p, tm=tm, name="mla_in")
    pos = jnp.concatenate([jnp.arange(L), jnp.tile(past + jnp.arange(T), Bs)])
    tabs = _rope_tables_packed(pos, MLA_THETA, R, R)
    tp = _tile(math.gcd(L, Ns), 256)
    lpt, npt = L // tp, Np // tp
    tab_spec = pl.BlockSpec((tp, LANES), lambda i: (jnp.where(i < npt, i % lpt, lpt + i - npt), 0))
    rowspec = lambda w: pl.BlockSpec((tp, w), lambda i: (i, 0))
    vec = lambda w: pl.BlockSpec((1, w), lambda i: (0, 0))
    tile_gain = lambda g: jnp.tile(g, LANES // R).reshape(1, LANES)
    shapes = [(QL, BF16), (KV, F32), (KV, BF16), (R, F32), (R, BF16)]
    cq, ckv, ckvb, kpe, kpeb = pl.pallas_call(
        functools.partial(_mla_prep_kernel, q_lora=QL, kv_lora=KV, rope=R),
        grid=(N // tp,),
        in_specs=[rowspec(proj.shape[1]), vec(QL), vec(KV), vec(LANES)] + [tab_spec] * 3,
        out_specs=[rowspec(w) for w, _ in shapes],
        out_shape=[jax.ShapeDtypeStruct((N, w), dt) for w, dt in shapes],
        compiler_params=_cp("parallel"),
        name="mla_prep",
    )(proj, qa_norm.reshape(1, QL), kv_norm.reshape(1, KV), tile_gain(kpe_norm), *tabs)

    w3 = w_uq.reshape(QL, H, NOPE + R)
    w_uq_p = jnp.concatenate([w3[:, :, :NOPE].reshape(QL, H * NOPE), w3[:, :, NOPE:].reshape(QL, H * R)], 1)
    qraw = _matmul(cq, w_uq_p, tm=tm, name="mla_uq")
    q_nope, q_pe = pl.pallas_call(
        functools.partial(_mla_q_kernel, heads=H, nope=NOPE, rope=R),
        grid=(N // tp,),
        in_specs=[rowspec(qraw.shape[1]), vec(NOPE), vec(LANES)] + [tab_spec] * 3,
        out_specs=[rowspec(H * NOPE), rowspec(H * R)],
        out_shape=[jax.ShapeDtypeStruct((N, H * NOPE), BF16), jax.ShapeDtypeStruct((N, H * R), BF16)],
        compiler_params=_cp("parallel"),
        name="mla_q",
    )(qraw, qn_nope.reshape(1, NOPE), tile_gain(qn_pe), *tabs)

    tk = _tile(Np, 512)
    k_nope = _matmul(ckvb[:Np], w_uk.reshape(KV, H * NOPE), tm=tk, out_dtype=BF16, name="mla_uk")
    v = _matmul(ckvb[:Np], w_uv.reshape(KV, H * VD), tm=tk, out_dtype=BF16, name="mla_uv")
    tq = Q_BLOCK
    nq = L // tq
    o_p = pl.pallas_call(
        functools.partial(_mla_prompt_kernel, heads=H, nope=NOPE, rope=R, vdim=VD, scale=scale),
        grid=(B, nq),
        in_specs=[
            pl.BlockSpec((tq, H * NOPE), lambda b, i: (b * nq + i, 0)),
            pl.BlockSpec((tq, H * R), lambda b, i: (b * nq + i, 0)),
            pl.BlockSpec((L, H * NOPE), lambda b, i: (b, 0)),
            pl.BlockSpec((L, LANES), lambda b, i: (b, 0)),
            pl.BlockSpec((L, H * VD), lambda b, i: (b, 0)),
        ],
        out_specs=pl.BlockSpec((tq, H * VD), lambda b, i: (b * nq + i, 0)),
        out_shape=jax.ShapeDtypeStruct((Np, H * VD), BF16),
        compiler_params=_cp("parallel", "arbitrary"),
        name="mla_prompt",
    )(q_nope, q_pe, k_nope, jnp.tile(kpeb[:Np], (1, LANES // R)), v)

    ts = _tile(Ns, 512)
    q_lat = _head_mm(q_nope[Np:], w_uk.reshape(KV, H * NOPE), H, trans_w=True, tm=ts, out_dtype=BF16,
                     name="mla_absorb_q")
    n_pages = page_table.shape[1]
    pc = _tile(n_pages, 8, 1)
    pad_new = lambda x: jnp.pad(x[Np:].reshape(Bs, T, x.shape[1]), ((0, 0), (0, LANES - T), (0, 0)))
    o_lat = _paged_call(
        functools.partial(_mla_sample_body, t=T, heads=H, scale=scale),
        page_table, [cache_ckv, cache_kpe],
        [q_lat.reshape(Bs, T * H, KV), q_pe[Np:].reshape(Bs, T * H, R), pad_new(ckvb), pad_new(kpeb)],
        [pl.BlockSpec((1, T * H, KV), lambda b, c, pt: (b, 0, 0)),
         pl.BlockSpec((1, T * H, R), lambda b, c, pt: (b, 0, 0)),
         pl.BlockSpec((1, LANES, KV), lambda b, c, pt: (b, 0, 0)),
         pl.BlockSpec((1, LANES, R), lambda b, c, pt: (b, 0, 0))],
        jax.ShapeDtypeStruct((Bs, T * H, KV), BF16),
        pl.BlockSpec((1, T * H, KV), lambda b, c, pt: (b, 0, 0)),
        [pltpu.VMEM((1, T * H, 1), F32), pltpu.VMEM((1, T * H, 1), F32), pltpu.VMEM((1, T * H, KV), F32)],
        pc, "mla_sample")
    o_s = _head_mm(o_lat.reshape(Ns, H * KV), w_uv.reshape(KV, H * VD), H, trans_w=False, tm=ts,
                   out_dtype=BF16, name="mla_absorb_o")
    return jnp.concatenate([o_p, o_s], 0), ckv, kpe


def _moe_kernel(be_ref, first_ref, nblk_ref, x_ref, g_ref, w1_ref, w3_ref, w2_ref, o_ref, w1b, w3b, w2b):
    i = pl.program_id(0)

    @pl.when(first_ref[i] == 1)
    def _():
        w1b[...] = w1_ref[0].astype(BF16)
        w3b[...] = w3_ref[0].astype(BF16)
        w2b[...] = w2_ref[0].astype(BF16)

    @pl.when(i < nblk_ref[0])
    def _():
        x = x_ref[...]
        h = jax.nn.silu(jnp.dot(x, w1b[...], preferred_element_type=F32)) * jnp.dot(x, w3b[...], preferred_element_type=F32)
        y = jnp.dot(h.astype(BF16), w2b[...], preferred_element_type=F32)
        o_ref[...] = y * g_ref[...]

    @pl.when(i >= nblk_ref[0])
    def _():
        o_ref[...] = jnp.zeros_like(o_ref)


def _moe(h, xn, w_group, b_group, w_expert, b_expert, w1, w3, w2, tm):
    N, D = xn.shape
    G = w_group.shape[1]
    E = w_expert.shape[1]
    EPG = E // G
    FF = w1.shape[2]
    R = MOE_ROWS
    logits = _matmul(xn, jnp.concatenate([w_group, w_expert], 1), tm=tm, name="moe_router")
    tok = jnp.arange(N)
    lg = logits[:, :G] + b_group
    grp = jnp.argmax(lg, -1)
    p_grp = jax.nn.softmax(lg, -1)[tok, grp]
    le = (logits[:, G:] + b_expert).reshape(N, G, EPG)[tok, grp]
    top_v, top_i = lax.top_k(le, MOE_TOPK)
    gates = (p_grp[:, None] * jax.nn.softmax(top_v, -1)).reshape(-1)
    experts = (grp[:, None] * EPG + top_i).reshape(-1).astype(jnp.int32)
    A = N * MOE_TOPK
    tok_flat = jnp.repeat(tok, MOE_TOPK).astype(jnp.int32)
    order = jnp.argsort(experts, stable=True)
    e_s, t_s, g_s = experts[order], tok_flat[order], gates[order]
    counts = jnp.bincount(experts, length=E)
    starts = jnp.cumsum(counts) - counts
    pcounts = (counts + R - 1) // R * R
    pends = jnp.cumsum(pcounts)
    pstarts = pends - pcounts
    n_blocks = -(-A // R) + E
    P = n_blocks * R
    blk_e = jnp.minimum(jnp.searchsorted(pends, jnp.arange(n_blocks) * R, side="right"), E - 1).astype(jnp.int32)
    n_used = (pends[-1] // R).astype(jnp.int32).reshape(1)
    first = jnp.concatenate([jnp.ones((1,), jnp.int32), (blk_e[1:] != blk_e[:-1]).astype(jnp.int32)])
    p = jnp.arange(P)
    pe = blk_e[p // R]
    within = p - pstarts[pe]
    valid = (within < counts[pe]) & (p < pends[-1])
    src = jnp.clip(starts[pe] + within, 0, A - 1)
    tok_pad = jnp.where(valid, t_s[src], 0)
    gate_pad = jnp.where(valid, g_s[src], 0.0)
    dest_sorted = pstarts[e_s] + jnp.arange(A) - starts[e_s]
    dest = jnp.zeros((A,), jnp.int32).at[order].set(dest_sorted.astype(jnp.int32))
    xg = xn[tok_pad]
    grid_spec = pltpu.PrefetchScalarGridSpec(
        num_scalar_prefetch=3,
        grid=(n_blocks,),
        in_specs=[
            pl.BlockSpec((R, D), lambda i, be, fi, nb: (i, 0)),
            pl.BlockSpec((R, 1), lambda i, be, fi, nb: (i, 0)),
            pl.BlockSpec((1, D, FF), lambda i, be, fi, nb: (be[i], 0, 0)),
            pl.BlockSpec((1, D, FF), lambda i, be, fi, nb: (be[i], 0, 0)),
            pl.BlockSpec((1, FF, D), lambda i, be, fi, nb: (be[i], 0, 0)),
        ],
        out_specs=pl.BlockSpec((R, D), lambda i, be, fi, nb: (i, 0)),
        scratch_shapes=[pltpu.VMEM((D, FF), BF16), pltpu.VMEM((D, FF), BF16), pltpu.VMEM((FF, D), BF16)],
    )
    yb = pl.pallas_call(
        _moe_kernel, grid_spec=grid_spec, out_shape=jax.ShapeDtypeStruct((P, D), F32),
        compiler_params=_cp("arbitrary"), name="moe_experts",
    )(blk_e, first, n_used, xg, gate_pad.reshape(P, 1), w1, w3, w2)
    d2 = dest.reshape(N, MOE_TOPK)
    return h + (yb[d2[:, 0]] + yb[d2[:, 1]])


def kernel(x_prompt, x_sample, state_ret, cache_k_c, cache_v_c, cache_kidx_c, cache_ckv_d, cache_kpe_d, page_table, norm_mix, norm_ffn, ret_w_in, ret_gn, ret_w_o, cm_w_in, cm_ln_g, cm_ln_b, cm_w_s, cm_b_s, cm_w_o, dsa_w_in, dsa_q_norm, dsa_k_norm, dsa_w_o, mla_w_in, mla_qa_norm, mla_kv_norm, mla_w_uq, mla_qn_nope, mla_qn_pe, mla_kpe_norm, mla_w_uk, mla_w_uv, mla_w_o, moe_w_group, moe_b_group, moe_w_expert, moe_b_expert, moe_w1, moe_w3, moe_w2):
    B, L, D = x_prompt.shape
    Bs, T, _ = x_sample.shape
    past = page_table.shape[1] * cache_k_c.shape[2]
    dims = (B, L, Bs, T, past)
    Np, Ns = B * L, Bs * T
    N = Np + Ns
    tm = _tile(N, 512, 16)
    depth = norm_mix.shape[0]
    h = jnp.concatenate([x_prompt.reshape(Np, D), x_sample.reshape(Ns, D)], 0)
    outs = {k: [] for k in ("ret_p", "ret_s", "cm_s", "kc", "vc", "ic", "ckv", "kpe")}
    for i in range(depth):
        kind, j = i % 4, i // 4
        xn = _rmsnorm(h, norm_mix[i], tm)
        if kind == 0:
            proj = _matmul(xn, ret_w_in[j], tm=tm, name="ret_in")
            o, s_p, s_s = _retention(proj, state_ret[j], ret_gn[j], dims)
            outs["ret_p"].append(s_p)
            outs["ret_s"].append(s_s)
            w_o = ret_w_o[j]
        elif kind == 1:
            uv = _matmul(xn, cm_w_in[j], tm=tm, act="gelu", name="cm_in")
            o, vn = _cm_gate(uv, cm_ln_g[j], cm_ln_b[j], cm_w_s[j], cm_b_s[j], dims)
            outs["cm_s"].append(vn.reshape(Bs, T, -1))
            w_o = cm_w_o[j]
        elif kind == 2:
            proj = _matmul(xn, dsa_w_in[j], tm=tm, name="dsa_in")
            o, k, v, ki = _dsa(proj, (cache_k_c[j], cache_v_c[j], cache_kidx_c[j]), page_table,
                               dsa_q_norm[j], dsa_k_norm[j], dims, dsa_w_o.shape[1] // cache_k_c.shape[-1])
            outs["kc"].append(k)
            outs["vc"].append(v)
            outs["ic"].append(ki)
            w_o = dsa_w_o[j]
        else:
            o, ckv, kpe = _mla(xn, (cache_ckv_d[j], cache_kpe_d[j]), page_table, mla_w_in[j], mla_qa_norm[j],
                               mla_kv_norm[j], mla_w_uq[j], mla_qn_nope[j], mla_qn_pe[j], mla_kpe_norm[j],
                               mla_w_uk[j], mla_w_uv[j], dims, tm)
            outs["ckv"].append(ckv)
            outs["kpe"].append(kpe)
            w_o = mla_w_o[j]
        h = _matmul(o, w_o, tm=tm, residual=h, name="mix_out")
        xn = _rmsnorm(h, norm_ffn[i], tm)
        h = _moe(h, xn, moe_w_group[i], moe_b_group[i], moe_w_expert[i], moe_b_expert[i],
                 moe_w1[i], moe_w3[i], moe_w2[i], tm)

    KVH, HD = cache_k_c.shape[3:]
    stack_p = lambda xs, shp: jnp.stack([x[:Np].reshape((B, L) + shp) for x in xs])
    stack_s = lambda xs, shp: jnp.stack([x[Np:].reshape((Bs, T) + shp) for x in xs])
    return (h[:Np].reshape(B, L, D), h[Np:].reshape(Bs, T, D),
            jnp.stack(outs["ret_p"]), jnp.stack(outs["ret_s"]), jnp.stack(outs["cm_s"]),
            stack_p(outs["kc"], (KVH, HD)), stack_p(outs["vc"], (KVH, HD)), stack_p(outs["ic"], (cache_kidx_c.shape[-1],)),
            stack_s(outs["kc"], (KVH, HD)), stack_s(outs["vc"], (KVH, HD)), stack_s(outs["ic"], (cache_kidx_c.shape[-1],)),
            stack_p(outs["ckv"], (cache_ckv_d.shape[-1],)), stack_p(outs["kpe"], (cache_kpe_d.shape[-1],)),
            stack_s(outs["ckv"], (cache_ckv_d.shape[-1],)), stack_s(outs["kpe"], (cache_kpe_d.shape[-1],)))
```

```python
import functools
import math

import jax
import jax.numpy as jnp
from jax import lax
from jax.experimental import pallas as pl
from jax.experimental.pallas import tpu as pltpu

F32 = jnp.float32
BF16 = jnp.bfloat16
EPS = 1e-6
LANES = 128
VMEM_LIMIT = 56 * 1024 * 1024
NEG = -1e30

Q_BLOCK = 128
IDX_TOPK = 256
RET_THETA = 10000.0
DSA_THETA = 500000.0
MLA_THETA = 10000.0
MOE_TOPK = 2
MOE_ROWS = 128


def _cp(*sem):
    return pltpu.CompilerParams(dimension_semantics=sem, vmem_limit_bytes=VMEM_LIMIT)


def _tile(n, pref, mult=8):
    best = None
    for d in range(mult, min(n, pref) + 1, mult):
        if n % d == 0:
            best = d
    assert best is not None, (n, pref, mult)
    return best


def _rmsnorm_kernel(x_ref, g_ref, o_ref):
    x = x_ref[...]
    y = x * lax.rsqrt(jnp.mean(x * x, -1, keepdims=True) + EPS)
    o_ref[...] = (y * g_ref[...]).astype(o_ref.dtype)


def _rmsnorm(h, g, tm, out_dtype=None):
    n, d = h.shape
    out_dtype = out_dtype or BF16
    return pl.pallas_call(
        _rmsnorm_kernel,
        grid=(n // tm,),
        in_specs=[pl.BlockSpec((tm, d), lambda i: (i, 0)), pl.BlockSpec((1, d), lambda i: (0, 0))],
        out_specs=pl.BlockSpec((tm, d), lambda i: (i, 0)),
        out_shape=jax.ShapeDtypeStruct((n, d), out_dtype),
        compiler_params=_cp("parallel"),
        name="rmsnorm",
    )(h, g.reshape(1, d))


def _mm_kernel(*refs, act, has_res):
    if has_res:
        a_ref, w_ref, r_ref, o_ref, wb_ref = refs
    else:
        a_ref, w_ref, o_ref, wb_ref = refs

    @pl.when(pl.program_id(1) == 0)
    def _():
        wb_ref[...] = w_ref[...].astype(BF16)

    acc = jnp.dot(a_ref[...].astype(BF16), wb_ref[...], preferred_element_type=F32)
    if act == "gelu":
        acc = jax.nn.gelu(acc)
    if has_res:
        acc = acc + r_ref[...]
    o_ref[...] = acc.astype(o_ref.dtype)


def _matmul(a, w, *, tm, out_dtype=F32, act=None, residual=None, name="matmul"):
    m, k = a.shape
    n = w.shape[1]
    tn_max = 1024 if k <= 2048 else (512 if k <= 4096 else 256)
    tn = n if n <= tn_max else tn_max
    in_specs = [pl.BlockSpec((tm, k), lambda j, i: (i, 0)), pl.BlockSpec((k, tn), lambda j, i: (0, j))]
    args = [a, w]
    if residual is not None:
        in_specs.append(pl.BlockSpec((tm, tn), lambda j, i: (i, j)))
        args.append(residual)
    return pl.pallas_call(
        functools.partial(_mm_kernel, act=act, has_res=residual is not None),
        grid=(pl.cdiv(n, tn), m // tm),
        in_specs=in_specs,
        out_specs=pl.BlockSpec((tm, tn), lambda j, i: (i, j)),
        out_shape=jax.ShapeDtypeStruct((m, n), out_dtype),
        scratch_shapes=[pltpu.VMEM((k, tn), BF16)],
        compiler_params=_cp("parallel", "arbitrary"),
        name=name,
    )(*args)


def _head_mm_kernel(a_ref, w_ref, o_ref, wb_ref, *, trans_w):
    @pl.when(pl.program_id(1) == 0)
    def _():
        wb_ref[...] = w_ref[...].astype(BF16)

    a = a_ref[...].astype(BF16)
    if trans_w:
        acc = lax.dot_general(a, wb_ref[...], (((1,), (1,)), ((), ())), preferred_element_type=F32)
    else:
        acc = jnp.dot(a, wb_ref[...], preferred_element_type=F32)
    o_ref[...] = acc.astype(o_ref.dtype)


def _head_mm(a, w, heads, *, trans_w, tm, out_dtype, name):
    m = a.shape[0]
    ka = a.shape[1] // heads
    if trans_w:
        nw = w.shape[0]
        wblk = (nw, ka)
    else:
        nw = w.shape[1] // heads
        wblk = (ka, nw)
    return pl.pallas_call(
        functools.partial(_head_mm_kernel, trans_w=trans_w),
        grid=(heads, m // tm),
        in_specs=[pl.BlockSpec((tm, ka), lambda h, i: (i, h)), pl.BlockSpec(wblk, lambda h, i: (0, h))],
        out_specs=pl.BlockSpec((tm, nw), lambda h, i: (i, h)),
        out_shape=jax.ShapeDtypeStruct((m, heads * nw), out_dtype),
        scratch_shapes=[pltpu.VMEM(wblk, BF16)],
        compiler_params=_cp("parallel", "arbitrary"),
        name=name,
    )(a, w)


def _rope_tables_full(pos, theta, n_rot):
    half = n_rot // 2
    inv = theta ** (-jnp.arange(half, dtype=F32) / half)
    ang = pos.astype(F32)[:, None] * inv[None, :]
    return jnp.cos(ang), jnp.sin(ang)


def _rope_tables_packed(pos, theta, n_rot, group):
    half = n_rot // 2
    cos, sin = _rope_tables_full(pos, theta, n_rot)
    lane = jnp.arange(LANES)
    within = lane % group
    fidx = within % half
    c = jnp.where(within < n_rot, cos[:, fidx], 1.0)
    s1 = jnp.where(within < half, -sin[:, fidx], 0.0)
    s2 = jnp.where((within >= half) & (within < n_rot), sin[:, fidx], 0.0)
    return c.astype(F32), s1.astype(F32), s2.astype(F32)


def _rope_packed(y, c, s1, s2, half):
    return y * c + pltpu.roll(y, LANES - half, 1) * s1 + pltpu.roll(y, half, 1) * s2


def _rope_split(x, cos, sin):
    half = cos.shape[-1]
    x1, x2 = x[:, :half], x[:, half:]
    return jnp.concatenate([x1 * cos - x2 * sin, x2 * cos + x1 * sin], -1)


def _groupnorm_gate(o, gn, g):
    mu = jnp.mean(o, -1, keepdims=True)
    d = o - mu
    var = jnp.mean(d * d, -1, keepdims=True)
    return (d * lax.rsqrt(var + EPS) * gn) * jax.nn.silu(g)


def _ret_prompt_kernel(q_ref, k_ref, v_ref, g_ref, cos_ref, sin_ref, din_ref, dq_ref, dk_ref, dch_ref,
                       gn_ref, o_ref, s_ref, *, scale):
    @pl.when(pl.program_id(2) == 0)
    def _():
        s_ref[...] = jnp.zeros_like(s_ref)

    cos, sin = cos_ref[...], sin_ref[...]
    q = _rope_split(q_ref[...], cos, sin)
    k = _rope_split(k_ref[...], cos, sin) * scale
    qb, kb, vb = q.astype(BF16), k.astype(BF16), v_ref[...].astype(BF16)
    state = s_ref[0, 0]
    att = lax.dot_general(qb, kb, (((1,), (1,)), ((), ())), preferred_element_type=F32) * din_ref[0]
    o = jnp.dot(att.astype(BF16), vb, preferred_element_type=F32)
    o = o + jnp.dot(qb, state.astype(BF16), preferred_element_type=F32) * dq_ref[0]
    kdt = (k * dk_ref[0]).T.astype(BF16)
    s_ref[0, 0] = state * dch_ref[0] + jnp.dot(kdt, vb, preferred_element_type=F32)
    o_ref[...] = _groupnorm_gate(o, gn_ref[0], g_ref[...]).astype(o_ref.dtype)


def _ret_sample_kernel(q_ref, k_ref, v_ref, g_ref, cos_ref, sin_ref, din_ref, dq_ref, dk_ref, dch_ref,
                       gn_ref, s0_ref, o_ref, s_ref, *, scale, nb, t):
    rows = nb * t
    cos, sin = cos_ref[...], sin_ref[...]
    q = _rope_split(q_ref[...], cos, sin)
    k = _rope_split(k_ref[...], cos, sin) * scale
    qb, kb, vb = q.astype(BF16), k.astype(BF16), v_ref[...].astype(BF16)
    att = lax.dot_general(qb, kb, (((1,), (1,)), ((), ())), preferred_element_type=F32) * din_ref[0]
    o = jnp.dot(att.astype(BF16), vb, preferred_element_type=F32)
    kd = k * dk_ref[0]
    pad = LANES - rows
    kdt = jnp.concatenate([kd, jnp.zeros((pad, kd.shape[1]), F32)], 0).T
    vpad = jnp.concatenate([vb, jnp.zeros((pad, vb.shape[1]), BF16)], 0)
    rowb = lax.broadcasted_iota(jnp.int32, (rows, 1), 0) // t
    colb = lax.broadcasted_iota(jnp.int32, (1, LANES), 1) // t
    dq = dq_ref[0]
    dch = dch_ref[0]
    for i in range(nb):
        state = s0_ref[0, i, 0]
        cross = jnp.dot(qb, state.astype(BF16), preferred_element_type=F32) * dq
        o = o + jnp.where(rowb == i, cross, 0.0)
        kdt_i = jnp.where(colb == i, kdt, 0.0).astype(BF16)
        s_ref[i, 0] = state * dch + jnp.dot(kdt_i, vpad, preferred_element_type=F32)
    o_ref[...] = _groupnorm_gate(o, gn_ref[0], g_ref[...]).astype(o_ref.dtype)


def _retention(proj, state_s, layer, gn, dims):
    B, L, Bs, T, past = dims
    Np, Ns = B * L, Bs * T
    H, DK, DV = state_s.shape[2:]
    half = DK // 2
    scale = DK ** -0.5
    vblk0 = (2 * H * DK) // DV
    log_gamma = jnp.log1p(-(2.0 ** (-5.0 - jnp.arange(H, dtype=F32))))

    def decay(C):
        idx = jnp.arange(C, dtype=F32)
        rel = idx[:, None] - idx[None, :]
        d_inner = jnp.where(rel >= 0, jnp.exp(log_gamma[:, None, None] * jnp.maximum(rel, 0.0)), 0.0)
        d_query = jnp.exp((idx[None, :] + 1.0) * log_gamma[:, None])[:, :, None]
        d_key = jnp.exp((C - 1.0 - idx[None, :]) * log_gamma[:, None])[:, :, None]
        d_chunk = jnp.broadcast_to(jnp.exp(C * log_gamma)[:, None, None], (H, 1, DV))
        return d_inner, d_query, d_key, d_chunk

    gn3 = gn.reshape(H, 1, DV)
    C = 128 if L % 128 == 0 else L
    n = L // C
    cos, sin = _rope_tables_full(jnp.arange(L), RET_THETA, DK)
    d_inner, d_query, d_key, d_chunk = decay(C)
    row = lambda b, h, c: b * n + c
    o_p, s_p = pl.pallas_call(
        functools.partial(_ret_prompt_kernel, scale=scale),
        grid=(B, H, n),
        in_specs=[
            pl.BlockSpec((C, DK), lambda b, h, c: (row(b, h, c), h)),
            pl.BlockSpec((C, DK), lambda b, h, c: (row(b, h, c), H + h)),
            pl.BlockSpec((C, DV), lambda b, h, c: (row(b, h, c), vblk0 + h)),
            pl.BlockSpec((C, DV), lambda b, h, c: (row(b, h, c), vblk0 + H + h)),
            pl.BlockSpec((C, half), lambda b, h, c: (c, 0)),
            pl.BlockSpec((C, half), lambda b, h, c: (c, 0)),
            pl.BlockSpec((1, C, C), lambda b, h, c: (h, 0, 0)),
            pl.BlockSpec((1, C, 1), lambda b, h, c: (h, 0, 0)),
            pl.BlockSpec((1, C, 1), lambda b, h, c: (h, 0, 0)),
            pl.BlockSpec((1, 1, DV), lambda b, h, c: (h, 0, 0)),
            pl.BlockSpec((1, 1, DV), lambda b, h, c: (h, 0, 0)),
        ],
        out_specs=[
            pl.BlockSpec((C, DV), lambda b, h, c: (row(b, h, c), h)),
            pl.BlockSpec((1, 1, DK, DV), lambda b, h, c: (b, h, 0, 0)),
        ],
        out_shape=[jax.ShapeDtypeStruct((Np, H * DV), BF16), jax.ShapeDtypeStruct((B, H, DK, DV), F32)],
        compiler_params=_cp("parallel", "parallel", "arbitrary"),
        name="retention_prompt",
    )(proj, proj, proj, proj, cos, sin, d_inner, d_query, d_key, d_chunk, gn3)

    nb = _tile(Bs, 8, 1)
    while (nb * T) % 8 or Np % (nb * T):
        nb -= 1
    rows = nb * T
    cos_s, sin_s = _rope_tables_full(past + jnp.arange(T), RET_THETA, DK)
    cos_s, sin_s = jnp.tile(cos_s, (nb, 1)), jnp.tile(sin_s, (nb, 1))
    di, dqs, dks, dchs = decay(T)
    same = (jnp.arange(rows)[:, None] // T) == (jnp.arange(rows)[None, :] // T)
    di = jnp.where(same[None], jnp.tile(di, (1, nb, nb)), 0.0)
    dqs, dks = jnp.tile(dqs, (1, nb, 1)), jnp.tile(dks, (1, nb, 1))
    r0 = Np // rows
    o_s, s_s = pl.pallas_call(
        functools.partial(_ret_sample_kernel, scale=scale, nb=nb, t=T),
        grid=(Bs // nb, H),
        in_specs=[
            pl.BlockSpec((rows, DK), lambda b, h: (r0 + b, h)),
            pl.BlockSpec((rows, DK), lambda b, h: (r0 + b, H + h)),
            pl.BlockSpec((rows, DV), lambda b, h: (r0 + b, vblk0 + h)),
            pl.BlockSpec((rows, DV), lambda b, h: (r0 + b, vblk0 + H + h)),
            pl.BlockSpec((rows, half), lambda b, h: (0, 0)),
            pl.BlockSpec((rows, half), lambda b, h: (0, 0)),
            pl.BlockSpec((1, rows, rows), lambda b, h: (h, 0, 0)),
            pl.BlockSpec((1, rows, 1), lambda b, h: (h, 0, 0)),
            pl.BlockSpec((1, rows, 1), lambda b, h: (h, 0, 0)),
            pl.BlockSpec((1, 1, DV), lambda b, h: (h, 0, 0)),
            pl.BlockSpec((1, 1, DV), lambda b, h: (h, 0, 0)),
            pl.BlockSpec((1, nb, 1, DK, DV), lambda b, h: (layer, b, h, 0, 0)),
        ],
        out_specs=[
            pl.BlockSpec((rows, DV), lambda b, h: (b, h)),
            pl.BlockSpec((nb, 1, DK, DV), lambda b, h: (b, h, 0, 0)),
        ],
        out_shape=[jax.ShapeDtypeStruct((Ns, H * DV), BF16), jax.ShapeDtypeStruct((Bs, H, DK, DV), F32)],
        compiler_params=_cp("parallel", "parallel"),
        name="retention_sample",
    )(proj, proj, proj, proj, cos_s, sin_s, di, dqs, dks, dchs, gn3, state_s)
    return jnp.concatenate([o_p, o_s], 0), s_p, s_s


def _cm_gate_kernel(u_ref, v_ref, lg_ref, lb_ref, ws_ref, bs_ref, y_ref, vn_ref, *, groups):
    v = v_ref[...]
    mu = jnp.mean(v, -1, keepdims=True)
    d = v - mu
    var = jnp.mean(d * d, -1, keepdims=True)
    vn = d * lax.rsqrt(var + EPS) * lg_ref[...] + lb_ref[...]
    vn_ref[...] = vn
    cw = v.shape[1] // groups
    for g in range(groups):
        sl = slice(g * cw, (g + 1) * cw)
        s = jnp.dot(ws_ref[0, g].astype(BF16), vn[:, sl].astype(BF16), preferred_element_type=F32) + bs_ref[0, g]
        y_ref[:, sl] = (u_ref[:, sl] * s).astype(y_ref.dtype)


def _cm_gate(uv, ln_g, ln_b, w_s, b_s, dims):
    B, L, Bs, T, past = dims
    Np, Ns = B * L, Bs * T
    W = uv.shape[1] // 2
    G, C = w_s.shape[0], w_s.shape[1]
    assert L % C == 0 and Ns % C == 0 and C % T == 0 and T <= C
    npc = Np // C
    r = jnp.arange(C)
    tril = r[:, None] >= r[None, :]
    ws_p = jnp.where(tril[None], w_s, 0.0)
    tt = r % T
    same = (r[:, None] // T) == (r[None, :] // T)
    ws_s = jnp.where((same & (tt[:, None] >= tt[None, :]))[None], w_s[:, tt[:, None], tt[None, :]], 0.0)
    ws2 = jnp.stack([ws_p, ws_s])
    bs2 = jnp.stack([b_s, b_s[:, tt]])[..., None]
    kind = lambda i: jnp.where(i < npc, 0, 1)
    y, vn = pl.pallas_call(
        functools.partial(_cm_gate_kernel, groups=G),
        grid=((Np + Ns) // C,),
        in_specs=[
            pl.BlockSpec((C, W), lambda i: (i, 0)),
            pl.BlockSpec((C, W), lambda i: (i, 1)),
            pl.BlockSpec((1, W), lambda i: (0, 0)),
            pl.BlockSpec((1, W), lambda i: (0, 0)),
            pl.BlockSpec((1, G, C, C), lambda i: (kind(i), 0, 0, 0)),
            pl.BlockSpec((1, G, C, 1), lambda i: (kind(i), 0, 0, 0)),
        ],
        out_specs=[
            pl.BlockSpec((C, W), lambda i: (i, 0)),
            pl.BlockSpec((C, W), lambda i: (jnp.maximum(i - npc, 0), 0)),
        ],
        out_shape=[jax.ShapeDtypeStruct((Np + Ns, W), BF16), jax.ShapeDtypeStruct((Ns, W), F32)],
        compiler_params=_cp("arbitrary"),
        name="cm_gate",
    )(uv, uv, ln_g.reshape(1, W), ln_b.reshape(1, W), ws2, bs2)
    return y, vn


def _topk_mask(sc, valid, k, idx_bits):
    int_min = jnp.int32(-2 ** 31)
    kf = jnp.float32(k)
    bits = pltpu.bitcast(sc + 0.0, jnp.int32)
    key = jnp.where(bits < 0, bits ^ jnp.int32(0x7FFFFFFF), bits)
    key = jnp.where(valid, key, int_min)

    def count(m):
        return jnp.sum(m.astype(F32), axis=1, keepdims=True)

    t0 = jnp.where(count(key >= 0) >= kf, jnp.int32(0), int_min)

    def value_bit(i, t):
        cand = t | (jnp.int32(1) << (30 - i))
        return jnp.where(count(key >= cand) >= kf, cand, t)

    thr = lax.fori_loop(0, 31, value_bit, t0)
    gt = key > thr
    eq = key == thr
    need = kf - count(gt)
    idx = lax.broadcasted_iota(jnp.int32, sc.shape, 1)

    def index_bit(i, x):
        cand = x | (jnp.int32(1) << (idx_bits - 1 - i))
        return jnp.where(count(eq & (idx < cand)) < need, cand, x)

    last = lax.fori_loop(0, idx_bits, index_bit, jnp.zeros_like(thr))
    return valid & (gt | (eq & (idx <= last)))


def _dsa_prep_kernel(x_ref, qn_ref, kn_ref, c_ref, s1_ref, s2_ref, ci_ref, si1_ref, si2_ref,
                     q_ref, k_ref, kb_ref, v_ref, vb_ref, qi_ref, ki_ref, kib_ref, wh_ref,
                     *, heads, kv_heads, idx_heads, half, half_i, wh_scale):
    hd = LANES
    c, s1, s2 = c_ref[...], s1_ref[...], s2_ref[...]
    ci, si1, si2 = ci_ref[...], si1_ref[...], si2_ref[...]

    def normed(x, gain):
        return x * lax.rsqrt(jnp.mean(x * x, -1, keepdims=True) + EPS) * gain

    off = 0
    for h in range(heads):
        y = _rope_packed(normed(x_ref[:, off:off + hd], qn_ref[...]), c, s1, s2, half)
        q_ref[:, h * hd:(h + 1) * hd] = y.astype(q_ref.dtype)
        off += hd
    for g in range(kv_heads):
        y = _rope_packed(normed(x_ref[:, off:off + hd], kn_ref[...]), c, s1, s2, half)
        k_ref[:, g * hd:(g + 1) * hd] = y
        kb_ref[:, g * hd:(g + 1) * hd] = y.astype(BF16)
        off += hd
    v = x_ref[:, off:off + kv_heads * hd]
    v_ref[...] = v
    vb_ref[...] = v.astype(BF16)
    off += kv_heads * hd
    for h in range(idx_heads):
        y = _rope_packed(x_ref[:, off:off + hd], ci, si1, si2, half_i)
        qi_ref[:, h * hd:(h + 1) * hd] = y.astype(qi_ref.dtype)
        off += hd
    y = _rope_packed(x_ref[:, off:off + hd], ci, si1, si2, half_i)
    ki_ref[...] = y
    kib_ref[...] = y.astype(BF16)
    off += hd
    wh_ref[...] = x_ref[:, off:off + idx_heads] * wh_scale


def _dsa_prompt_kernel(q_ref, qi_ref, wh_ref, k_ref, v_ref, ki_ref, o_ref, bias_ref,
                       *, kv_heads, rep, idx_heads, k_top, idx_bits, idx_scale, scale):
    hd = LANES
    tq, L = bias_ref.shape
    kib = ki_ref[...]
    wh = wh_ref[...] * idx_scale
    sc = jnp.zeros((tq, L), F32)
    for h in range(idx_heads):
        d = lax.dot_general(qi_ref[:, h * hd:(h + 1) * hd], kib, (((1,), (1,)), ((), ())),
                            preferred_element_type=F32)
        sc = sc + wh[:, h:h + 1] * jnp.maximum(d, 0.0)
    qpos = pl.program_id(1) * tq + lax.broadcasted_iota(jnp.int32, (tq, L), 0)
    spos = lax.broadcasted_iota(jnp.int32, (tq, L), 1)
    mask = _topk_mask(sc, spos <= qpos, k_top, idx_bits)
    bias_ref[...] = jnp.where(mask, 0.0, -jnp.inf)
    for g in range(kv_heads):
        kg = k_ref[:, g * hd:(g + 1) * hd]
        vg = v_ref[:, g * hd:(g + 1) * hd]
        for r in range(rep):
            hq = g * rep + r
            s = lax.dot_general(q_ref[:, hq * hd:(hq + 1) * hd], kg, (((1,), (1,)), ((), ())),
                                preferred_element_type=F32) * scale + bias_ref[...]
            e = jnp.exp(s - jnp.max(s, -1, keepdims=True))
            p = (e * (1.0 / jnp.sum(e, -1, keepdims=True))).astype(BF16)
            o_ref[:, hq * hd:(hq + 1) * hd] = jnp.dot(p, vg, preferred_element_type=F32).astype(o_ref.dtype)


def _paged_kernel(body, layer, planes, n_dense, n_out, pc, n_pages, page_rows):
    nchunks = n_pages // pc
    n_paged = len(planes)

    def kernel(pt_ref, *refs):
        paged = refs[:n_paged]
        dense = refs[n_paged:n_paged + n_dense]
        outs = refs[n_paged + n_dense:n_paged + n_dense + n_out]
        rest = refs[n_paged + n_dense + n_out:]
        bufs, sem, user = rest[:n_paged], rest[n_paged], rest[n_paged + 1:]
        c = pl.program_id(1)
        step = pl.program_id(0) * nchunks + c
        total = pl.num_programs(0) * nchunks
        slot = step % 2

        def copies(st, sl):
            base = (st // nchunks) * n_pages + (st % nchunks) * pc
            out = []
            for a in range(n_paged):
                for p in range(pc):
                    page = pt_ref[base + p]
                    rows = pl.ds(p * page_rows, page_rows)
                    if planes[a] is None:
                        out.append(pltpu.make_async_copy(paged[a].at[layer, page], bufs[a].at[sl, rows], sem.at[sl, a]))
                    elif planes[a] == "T":
                        out.append(pltpu.make_async_copy(paged[a].at[layer, page], bufs[a].at[sl, :, rows], sem.at[sl, a]))
                    else:
                        for g in range(planes[a]):
                            out.append(pltpu.make_async_copy(paged[a].at[layer, page, :, g],
                                                             bufs[a].at[sl, g, rows], sem.at[sl, a]))
            return out

        @pl.when(step == 0)
        def _():
            for cp in copies(step, slot):
                cp.start()

        @pl.when(step + 1 < total)
        def _():
            for cp in copies(step + 1, 1 - slot):
                cp.start()

        for cp in copies(step, slot):
            cp.wait()
        body(c, nchunks, [bufs[a].at[slot] for a in range(n_paged)], dense, outs, user)

    return kernel


def _paged_call(body, page_table, layer, paged, dense, dense_specs, out_shapes, out_specs, user_scratch, pc, name,
                transposed=()):
    Bs, n_pages = page_table.shape
    page_rows = paged[0].shape[2]
    nchunks = n_pages // pc
    planes = ["T" if i in transposed else (a.shape[3] if a.ndim == 5 else None) for i, a in enumerate(paged)]

    def window(a, g):
        if g is None:
            return (2, pc * page_rows, a.shape[3])
        if g == "T":
            return (2, a.shape[2], pc * page_rows)
        return (2, g, pc * page_rows, a.shape[4])

    kernel = _paged_kernel(body, layer, planes, len(dense), 1, pc, n_pages, page_rows)
    scratch = [pltpu.VMEM(window(a, g), a.dtype) for a, g in zip(paged, planes)]
    scratch.append(pltpu.SemaphoreType.DMA((2, len(paged))))
    scratch.extend(user_scratch)
    grid_spec = pltpu.PrefetchScalarGridSpec(
        num_scalar_prefetch=1,
        grid=(Bs, nchunks),
        in_specs=[pl.BlockSpec(memory_space=pl.ANY)] * len(paged) + list(dense_specs),
        out_specs=out_specs,
        scratch_shapes=scratch,
    )
    return pl.pallas_call(
        kernel, grid_spec=grid_spec, out_shape=out_shapes,
        compiler_params=_cp("arbitrary", "arbitrary"), name=name,
    )(page_table.reshape(-1), *paged, *dense)


def _dsa_sample_select_body(c, nchunks, bufs, dense, outs, user, *, t, idx_heads, k_top, idx_bits, idx_scale):
    (ki_buf,) = bufs
    qi_ref, wh_ref, kin_ref = dense
    (bias_ref,) = outs
    (sc_ref,) = user
    w = ki_buf.shape[0]
    past = nchunks * w

    def scores(keys_bf16):
        d = lax.dot_general(qi_ref[0], keys_bf16, (((1,), (1,)), ((), ())), preferred_element_type=F32)
        wd = jnp.maximum(d, 0.0) * (wh_ref[0] * idx_scale)
        return jnp.sum(wd.reshape(t, idx_heads, wd.shape[1]), axis=1)

    @pl.when(c == 0)
    def _():
        sc_ref[...] = jnp.zeros_like(sc_ref)

    sc_ref[0:t, pl.ds(pl.multiple_of(c * w, LANES), w)] = scores(ki_buf[...].astype(BF16))

    @pl.when(c == nchunks - 1)
    def _():
        sc_ref[0:t, past:past + LANES] = scores(kin_ref[0])
        rows, S = sc_ref.shape
        row = lax.broadcasted_iota(jnp.int32, (rows, S), 0)
        col = lax.broadcasted_iota(jnp.int32, (rows, S), 1)
        valid = (col < past) | ((col - past <= row) & (col - past < t))
        mask = _topk_mask(sc_ref[...], valid, k_top, idx_bits)
        bias_ref[0] = jnp.where(mask, 0.0, NEG)[0:t]


def _expand_rows(x, reps):
    t, w = x.shape
    row = lax.broadcasted_iota(jnp.int32, (t * reps, w), 0) // reps
    out = jnp.broadcast_to(x[0:1], (t * reps, w))
    for i in range(1, t):
        out = jnp.where(row == i, jnp.broadcast_to(x[i:i + 1], (t * reps, w)), out)
    return out


def _softmax_update(s, v_bf16, m_ref, l_ref, acc_ref, idx):
    m_old = m_ref[idx]
    m_new = jnp.maximum(m_old, jnp.max(s, -1, keepdims=True))
    a = jnp.exp(m_old - m_new)
    p = jnp.exp(s - m_new)
    l_ref[idx] = l_ref[idx] * a + jnp.sum(p, -1, keepdims=True)
    acc_ref[idx] = acc_ref[idx] * a + jnp.dot(p.astype(BF16), v_bf16, preferred_element_type=F32)
    m_ref[idx] = m_new


def _dsa_sample_attend_body(c, nchunks, bufs, dense, outs, user, *, kv_heads, rep, scale):
    k_buf, v_buf = bufs
    q_ref, bias_ref, tail_ref, kn_ref, vn_ref = dense
    (o_ref,) = outs
    m_ref, l_ref, acc_ref = user
    hd = LANES

    @pl.when(c == 0)
    def _():
        m_ref[...] = jnp.full_like(m_ref, NEG)
        l_ref[...] = jnp.zeros_like(l_ref)
        acc_ref[...] = jnp.zeros_like(acc_ref)

    def attend(keys, vals, bias):
        bias = _expand_rows(bias, rep)
        for g in range(kv_heads):
            s = lax.dot_general(q_ref[0, g], keys(g), (((1,), (1,)), ((), ())),
                                preferred_element_type=F32) * scale + bias
            _softmax_update(s, vals(g), m_ref, l_ref, acc_ref, g)

    attend(lambda g: k_buf[g].astype(BF16), lambda g: v_buf[g].astype(BF16), bias_ref[0])

    @pl.when(c == nchunks - 1)
    def _():
        attend(lambda g: kn_ref[0, :, g * hd:(g + 1) * hd], lambda g: vn_ref[0, :, g * hd:(g + 1) * hd], tail_ref[0])
        o_ref[0] = (acc_ref[...] / l_ref[...]).astype(o_ref.dtype)


def _dsa(proj, caches, layer, page_table, q_norm, k_norm, dims, dsa_heads):
    cache_k, cache_v, cache_ki = caches
    B, L, Bs, T, past = dims
    Np, Ns = B * L, Bs * T
    N = Np + Ns
    _, n_pool, page, KVH, HD = cache_k.shape
    IDX_DIM = cache_ki.shape[-1]
    H = dsa_heads
    REP = H // KVH
    IDXH = (proj.shape[1] - H * HD - 2 * KVH * HD - IDX_DIM) // (IDX_DIM + 1)
    assert HD == LANES and IDX_DIM == LANES
    rope_dims, idx_rope = HD // 4, IDX_DIM // 4
    idx_scale = IDX_DIM ** -0.5
    scale = HD ** -0.5

    pos = jnp.concatenate([jnp.arange(L), jnp.tile(past + jnp.arange(T), Bs)])
    tabs = _rope_tables_packed(pos, DSA_THETA, rope_dims, HD) + _rope_tables_packed(pos, DSA_THETA, idx_rope, IDX_DIM)
    tm = _tile(math.gcd(L, Ns), 256)
    lpt, npt = L // tm, Np // tm
    tab_spec = pl.BlockSpec((tm, LANES), lambda i: (jnp.where(i < npt, i % lpt, lpt + i - npt), 0))
    rowspec = lambda w: pl.BlockSpec((tm, w), lambda i: (i, 0))
    shapes = [(H * HD, BF16), (KVH * HD, F32), (KVH * HD, BF16), (KVH * HD, F32), (KVH * HD, BF16),
              (IDXH * IDX_DIM, BF16), (IDX_DIM, F32), (IDX_DIM, BF16), (IDXH, F32)]
    q, k, kb, v, vb, qi, ki, kib, wh = pl.pallas_call(
        functools.partial(_dsa_prep_kernel, heads=H, kv_heads=KVH, idx_heads=IDXH, half=rope_dims // 2,
                          half_i=idx_rope // 2, wh_scale=IDXH ** -0.5),
        grid=(N // tm,),
        in_specs=[rowspec(proj.shape[1]), pl.BlockSpec((1, HD), lambda i: (0, 0)),
                  pl.BlockSpec((1, HD), lambda i: (0, 0))] + [tab_spec] * 6,
        out_specs=[rowspec(w) for w, _ in shapes],
        out_shape=[jax.ShapeDtypeStruct((N, w), dt) for w, dt in shapes],
        compiler_params=_cp("parallel"),
        name="dsa_prep",
    )(proj, q_norm.reshape(1, HD), k_norm.reshape(1, HD), *tabs)

    tq = Q_BLOCK
    nq = L // tq
    k_top = min(IDX_TOPK, L // 4)
    o_p = pl.pallas_call(
        functools.partial(_dsa_prompt_kernel, kv_heads=KVH, rep=REP, idx_heads=IDXH, k_top=k_top,
                          idx_bits=max(1, (L - 1).bit_length()), idx_scale=idx_scale, scale=scale),
        grid=(B, nq),
        in_specs=[
            pl.BlockSpec((tq, H * HD), lambda b, i: (b * nq + i, 0)),
            pl.BlockSpec((tq, IDXH * IDX_DIM), lambda b, i: (b * nq + i, 0)),
            pl.BlockSpec((tq, IDXH), lambda b, i: (b * nq + i, 0)),
            pl.BlockSpec((L, KVH * HD), lambda b, i: (b, 0)),
            pl.BlockSpec((L, KVH * HD), lambda b, i: (b, 0)),
            pl.BlockSpec((L, IDX_DIM), lambda b, i: (b, 0)),
        ],
        out_specs=pl.BlockSpec((tq, H * HD), lambda b, i: (b * nq + i, 0)),
        out_shape=jax.ShapeDtypeStruct((Np, H * HD), BF16),
        scratch_shapes=[pltpu.VMEM((tq, L), F32)],
        compiler_params=_cp("parallel", "arbitrary"),
        name="dsa_prompt",
    )(q, qi, wh, kb, vb, kib)

    n_pages = page_table.shape[1]
    pc = _tile(n_pages, 16, 1)
    w = pc * page
    S = past + LANES
    k_top_s = min(IDX_TOPK, (past + T) // 4)
    rows = 8
    qi_s = qi[Np:].reshape(Bs, T * IDXH, IDX_DIM)
    wh_s = wh[Np:].reshape(Bs, T * IDXH, 1)
    pad_new = lambda x: jnp.pad(x[Np:].reshape(Bs, T, x.shape[1]), ((0, 0), (0, LANES - T), (0, 0)))
    bias = _paged_call(
        functools.partial(_dsa_sample_select_body, t=T, idx_heads=IDXH, k_top=k_top_s,
                          idx_bits=(S - 1).bit_length(), idx_scale=idx_scale),
        page_table, layer, [cache_ki],
        [qi_s, wh_s, pad_new(kib)],
        [pl.BlockSpec((1, T * IDXH, IDX_DIM), lambda b, c, pt: (b, 0, 0)),
         pl.BlockSpec((1, T * IDXH, 1), lambda b, c, pt: (b, 0, 0)),
         pl.BlockSpec((1, LANES, IDX_DIM), lambda b, c, pt: (b, 0, 0))],
        jax.ShapeDtypeStruct((Bs, T, S), F32),
        pl.BlockSpec((1, T, S), lambda b, c, pt: (b, 0, 0)),
        [pltpu.VMEM((rows, S), F32)],
        pc, "dsa_sample_select")

    q_s = q[Np:].reshape(Bs, T, KVH, REP, HD).transpose(0, 2, 1, 3, 4).reshape(Bs, KVH, T * REP, HD)
    o_s = _paged_call(
        functools.partial(_dsa_sample_attend_body, kv_heads=KVH, rep=REP, scale=scale),
        page_table, layer, [cache_k, cache_v],
        [q_s, bias, bias, pad_new(kb), pad_new(vb)],
        [pl.BlockSpec((1, KVH, T * REP, HD), lambda b, c, pt: (b, 0, 0, 0)),
         pl.BlockSpec((1, T, w), lambda b, c, pt: (b, 0, c)),
         pl.BlockSpec((1, T, LANES), lambda b, c, pt: (b, 0, past // LANES)),
         pl.BlockSpec((1, LANES, KVH * HD), lambda b, c, pt: (b, 0, 0)),
         pl.BlockSpec((1, LANES, KVH * HD), lambda b, c, pt: (b, 0, 0))],
        jax.ShapeDtypeStruct((Bs, KVH, T * REP, HD), BF16),
        pl.BlockSpec((1, KVH, T * REP, HD), lambda b, c, pt: (b, 0, 0, 0)),
        [pltpu.VMEM((KVH, T * REP, 1), F32), pltpu.VMEM((KVH, T * REP, 1), F32),
         pltpu.VMEM((KVH, T * REP, HD), F32)],
        pc, "dsa_sample_attend")
    o_s = o_s.reshape(Bs, KVH, T, REP, HD).transpose(0, 2, 1, 3, 4).reshape(Ns, H * HD)
    return jnp.concatenate([o_p, o_s], 0), k, v, ki


def _norm_rope_pairs(x, gain, c, s1, s2, group, half):
    lane = lax.broadcasted_iota(jnp.int32, x.shape, 1)
    x2 = x * x
    inv = jnp.zeros_like(x)
    for j in range(LANES // group):
        sel = (lane >= j * group) & (lane < (j + 1) * group)
        ms = jnp.sum(jnp.where(sel, x2, 0.0), -1, keepdims=True) * (1.0 / group)
        inv = jnp.where(sel, lax.rsqrt(ms + EPS), inv)
    return _rope_packed(x * inv * gain, c, s1, s2, half)


def _mla_prep_kernel(x_ref, qa_ref, kvn_ref, kpn_ref, c_ref, s1_ref, s2_ref,
                     cq_ref, ckv_ref, ckvb_ref, kpe_ref, kpeb_ref, *, q_lora, kv_lora, rope):
    def normed(x, gain):
        return x * lax.rsqrt(jnp.mean(x * x, -1, keepdims=True) + EPS) * gain

    cq_ref[...] = normed(x_ref[:, 0:q_lora], qa_ref[...]).astype(cq_ref.dtype)
    ckv = normed(x_ref[:, q_lora:q_lora + kv_lora], kvn_ref[...])
    ckv_ref[...] = ckv
    ckvb_ref[...] = ckv.astype(BF16)
    y = _norm_rope_pairs(x_ref[:, q_lora + kv_lora:q_lora + kv_lora + LANES], kpn_ref[...],
                         c_ref[...], s1_ref[...], s2_ref[...], rope, rope // 2)
    kpe_ref[...] = y[:, 0:rope]
    kpeb_ref[...] = y[:, 0:rope].astype(BF16)


def _mla_q_kernel(x_ref, gn_ref, gp_ref, c_ref, s1_ref, s2_ref, qn_ref, qp_ref, *, heads, nope, rope):
    def normed(x, gain):
        return x * lax.rsqrt(jnp.mean(x * x, -1, keepdims=True) + EPS) * gain

    for h in range(heads):
        qn_ref[:, h * nope:(h + 1) * nope] = normed(x_ref[:, h * nope:(h + 1) * nope], gn_ref[...]).astype(qn_ref.dtype)
    base = heads * nope
    c, s1, s2 = c_ref[...], s1_ref[...], s2_ref[...]
    for j in range(heads * rope // LANES):
        y = _norm_rope_pairs(x_ref[:, base + j * LANES:base + (j + 1) * LANES], gp_ref[...], c, s1, s2, rope, rope // 2)
        qp_ref[:, j * LANES:(j + 1) * LANES] = y.astype(qp_ref.dtype)


def _mla_prompt_kernel(qn_ref, qp_ref, kn_ref, kp_ref, v_ref, o_ref, *, heads, nope, rope, vdim, scale):
    tq = qn_ref.shape[0]
    L = kn_ref.shape[0]
    qpos = pl.program_id(1) * tq + lax.broadcasted_iota(jnp.int32, (tq, L), 0)
    spos = lax.broadcasted_iota(jnp.int32, (tq, L), 1)
    causal = spos <= qpos
    kp = kp_ref[...]
    dn = (((1,), (1,)), ((), ()))
    per_tile = LANES // rope
    lane = lax.broadcasted_iota(jnp.int32, (tq, LANES), 1)
    for h in range(heads):
        j, sub = h // per_tile, h % per_tile
        qp = qp_ref[:, j * LANES:(j + 1) * LANES]
        qp = jnp.where((lane >= sub * rope) & (lane < (sub + 1) * rope), qp, jnp.zeros_like(qp))
        s = (lax.dot_general(qn_ref[:, h * nope:(h + 1) * nope], kn_ref[:, h * nope:(h + 1) * nope], dn,
                             preferred_element_type=F32)
             + lax.dot_general(qp, kp, dn, preferred_element_type=F32)) * scale
        s = jnp.where(causal, s, -jnp.inf)
        e = jnp.exp(s - jnp.max(s, -1, keepdims=True))
        p = (e * (1.0 / jnp.sum(e, -1, keepdims=True))).astype(BF16)
        o_ref[:, h * vdim:(h + 1) * vdim] = jnp.dot(p, v_ref[:, h * vdim:(h + 1) * vdim],
                                                    preferred_element_type=F32).astype(o_ref.dtype)


def _mla_sample_body(c, nchunks, bufs, dense, outs, user, *, t, heads, scale):
    c_buf, r_buf = bufs
    ql_ref, qp_ref, cn_ref, rn_ref = dense
    (o_ref,) = outs
    m_ref, l_ref, acc_ref = user
    dn = (((1,), (1,)), ((), ()))

    @pl.when(c == 0)
    def _():
        m_ref[...] = jnp.full_like(m_ref, NEG)
        l_ref[...] = jnp.zeros_like(l_ref)
        acc_ref[...] = jnp.zeros_like(acc_ref)

    lat = c_buf[...].astype(BF16)
    s = (lax.dot_general(ql_ref[0], lat, dn, preferred_element_type=F32)
         + jnp.dot(qp_ref[0], r_buf[...].astype(BF16), preferred_element_type=F32)) * scale
    _softmax_update(s, lat, m_ref, l_ref, acc_ref, 0)

    @pl.when(c == nchunks - 1)
    def _():
        lat_n = cn_ref[0]
        s = (lax.dot_general(ql_ref[0], lat_n, dn, preferred_element_type=F32)
             + lax.dot_general(qp_ref[0], rn_ref[0], dn, preferred_element_type=F32)) * scale
        trow = lax.broadcasted_iota(jnp.int32, s.shape, 0) // heads
        col = lax.broadcasted_iota(jnp.int32, s.shape, 1)
        s = jnp.where((col <= trow) & (col < t), s, NEG)
        _softmax_update(s, lat_n, m_ref, l_ref, acc_ref, 0)
        o_ref[0] = (acc_ref[0] / l_ref[0]).astype(o_ref.dtype)


def _mla(xn, caches, layer, page_table, w_in, qa_norm, kv_norm, w_uq, qn_nope, qn_pe, kpe_norm, w_uk, w_uv, dims, tm):
    cache_ckv, cache_kpe = caches
    B, L, Bs, T, past = dims
    Np, Ns = B * L, Bs * T
    N = Np + Ns
    KV = cache_ckv.shape[-1]
    R = cache_kpe.shape[-1]
    QL = w_uq.shape[0]
    H, NOPE = w_uk.shape[1], w_uk.shape[2]
    VD = w_uv.shape[2]
    assert NOPE == LANES and LANES % R == 0 and (H * R) % LANES == 0 and (QL + KV) % LANES == 0
    scale = (NOPE + R) ** -0.5

    w_in_p = jnp.pad(w_in, ((0, 0), (0, LANES - R)))
    proj = _matmul(xn, w_in_p, tm=tm, name="mla_in")
    pos = jnp.concatenate([jnp.arange(L), jnp.tile(past + jnp.arange(T), Bs)])
    tabs = _rope_tables_packed(pos, MLA_THETA, R, R)
    tp = _tile(math.gcd(L, Ns), 256)
    lpt, npt = L // tp, Np // tp
    tab_spec = pl.BlockSpec((tp, LANES), lambda i: (jnp.where(i < npt, i % lpt, lpt + i - npt), 0))
    rowspec = lambda w: pl.BlockSpec((tp, w), lambda i: (i, 0))
    vec = lambda w: pl.BlockSpec((1, w), lambda i: (0, 0))
    tile_gain = lambda g: jnp.tile(g, LANES // R).reshape(1, LANES)
    shapes = [(QL, BF16), (KV, F32), (KV, BF16), (R, F32), (R, BF16)]
    cq, ckv, ckvb, kpe, kpeb = pl.pallas_call(
        functools.partial(_mla_prep_kernel, q_lora=QL, kv_lora=KV, rope=R),
        grid=(N // tp,),
        in_specs=[rowspec(proj.shape[1]), vec(QL), vec(KV), vec(LANES)] + [tab_spec] * 3,
        out_specs=[rowspec(w) for w, _ in shapes],
        out_shape=[jax.ShapeDtypeStruct((N, w), dt) for w, dt in shapes],
        compiler_params=_cp("parallel"),
        name="mla_prep",
    )(proj, qa_norm.reshape(1, QL), kv_norm.reshape(1, KV), tile_gain(kpe_norm), *tabs)

    w3 = w_uq.reshape(QL, H, NOPE + R)
    w_uq_p = jnp.concatenate([w3[:, :, :NOPE].reshape(QL, H * NOPE), w3[:, :, NOPE:].reshape(QL, H * R)], 1)
    qraw = _matmul(cq, w_uq_p, tm=tm, name="mla_uq")
    q_nope, q_pe = pl.pallas_call(
        functools.partial(_mla_q_kernel, heads=H, nope=NOPE, rope=R),
        grid=(N // tp,),
        in_specs=[rowspec(qraw.shape[1]), vec(NOPE), vec(LANES)] + [tab_spec] * 3,
        out_specs=[rowspec(H * NOPE), rowspec(H * R)],
        out_shape=[jax.ShapeDtypeStruct((N, H * NOPE), BF16), jax.ShapeDtypeStruct((N, H * R), BF16)],
        compiler_params=_cp("parallel"),
        name="mla_q",
    )(qraw, qn_nope.reshape(1, NOPE), tile_gain(qn_pe), *tabs)

    tk = _tile(Np, 512)
    k_nope = _matmul(ckvb[:Np], w_uk.reshape(KV, H * NOPE), tm=tk, out_dtype=BF16, name="mla_uk")
    v = _matmul(ckvb[:Np], w_uv.reshape(KV, H * VD), tm=tk, out_dtype=BF16, name="mla_uv")
    tq = Q_BLOCK
    nq = L // tq
    o_p = pl.pallas_call(
        functools.partial(_mla_prompt_kernel, heads=H, nope=NOPE, rope=R, vdim=VD, scale=scale),
        grid=(B, nq),
        in_specs=[
            pl.BlockSpec((tq, H * NOPE), lambda b, i: (b * nq + i, 0)),
            pl.BlockSpec((tq, H * R), lambda b, i: (b * nq + i, 0)),
            pl.BlockSpec((L, H * NOPE), lambda b, i: (b, 0)),
            pl.BlockSpec((L, LANES), lambda b, i: (b, 0)),
            pl.BlockSpec((L, H * VD), lambda b, i: (b, 0)),
        ],
        out_specs=pl.BlockSpec((tq, H * VD), lambda b, i: (b * nq + i, 0)),
        out_shape=jax.ShapeDtypeStruct((Np, H * VD), BF16),
        compiler_params=_cp("parallel", "arbitrary"),
        name="mla_prompt",
    )(q_nope, q_pe, k_nope, jnp.tile(kpeb[:Np], (1, LANES // R)), v)

    ts = _tile(Ns, 512)
    q_lat = _head_mm(q_nope[Np:], w_uk.reshape(KV, H * NOPE), H, trans_w=True, tm=ts, out_dtype=BF16,
                     name="mla_absorb_q")
    n_pages = page_table.shape[1]
    pc = _tile(n_pages, 16, 1)
    pad_new = lambda x: jnp.pad(x[Np:].reshape(Bs, T, x.shape[1]), ((0, 0), (0, LANES - T), (0, 0)))
    o_lat = _paged_call(
        functools.partial(_mla_sample_body, t=T, heads=H, scale=scale),
        page_table, layer, [cache_ckv, jnp.swapaxes(cache_kpe, 2, 3)],
        [q_lat.reshape(Bs, T * H, KV), q_pe[Np:].reshape(Bs, T * H, R), pad_new(ckvb), pad_new(kpeb)],
        [pl.BlockSpec((1, T * H, KV), lambda b, c, pt: (b, 0, 0)),
         pl.BlockSpec((1, T * H, R), lambda b, c, pt: (b, 0, 0)),
         pl.BlockSpec((1, LANES, KV), lambda b, c, pt: (b, 0, 0)),
         pl.BlockSpec((1, LANES, R), lambda b, c, pt: (b, 0, 0))],
        jax.ShapeDtypeStruct((Bs, T * H, KV), BF16),
        pl.BlockSpec((1, T * H, KV), lambda b, c, pt: (b, 0, 0)),
        [pltpu.VMEM((1, T * H, 1), F32), pltpu.VMEM((1, T * H, 1), F32), pltpu.VMEM((1, T * H, KV), F32)],
        pc, "mla_sample", transposed=(1,))
    o_s = _head_mm(o_lat.reshape(Ns, H * KV), w_uv.reshape(KV, H * VD), H, trans_w=False, tm=ts,
                   out_dtype=BF16, name="mla_absorb_o")
    return jnp.concatenate([o_p, o_s], 0), ckv, kpe


def _moe_count_kernel(e_ref, cnt_ref):
    @pl.when(pl.program_id(0) == 0)
    def _():
        cnt_ref[...] = jnp.zeros_like(cnt_ref)

    e = e_ref[0]
    hit = lax.broadcasted_iota(jnp.int32, (cnt_ref.shape[0], e.shape[1]), 0) == e
    cnt_ref[...] += jnp.sum(hit.astype(F32), axis=1, keepdims=True)


def _moe_dest_kernel(e_ref, pstart_ref, dest_ref, run_ref):
    @pl.when(pl.program_id(0) == 0)
    def _():
        run_ref[...] = pstart_ref[...]

    e = e_ref[0]
    ch = e.shape[1]
    hit = lax.broadcasted_iota(jnp.int32, (run_ref.shape[0], ch), 0) == e
    tri = (lax.broadcasted_iota(jnp.int32, (ch, ch), 0) <= lax.broadcasted_iota(jnp.int32, (ch, ch), 1))
    prefix = jnp.dot(hit.astype(BF16), tri.astype(BF16), preferred_element_type=F32)
    pos = prefix - 1.0 + run_ref[...]
    dest_ref[0] = jnp.sum(jnp.where(hit, pos, 0.0), axis=0, keepdims=True).astype(jnp.int32)
    run_ref[...] += jnp.sum(hit.astype(F32), axis=1, keepdims=True)


def _moe_kernel(be_ref, first_ref, nxt_ref, par_ref, nblk_ref, tok_ref, x_hbm, g_ref, w1_hbm, w3_hbm, w2_hbm,
                o_ref, xbuf, wf1, wf3, wf2, w1b, w3b, w2b, xsem, wsem, *, layer, rows):
    i = pl.program_id(0)
    n = nblk_ref[0]
    slot = i % 2

    def x_copy(blk, sl, r):
        return pltpu.make_async_copy(x_hbm.at[pl.ds(tok_ref[blk * rows + r], 1)], xbuf.at[sl, pl.ds(r, 1)], xsem.at[sl])

    def start_rows(blk, sl):
        def issue(r, carry):
            x_copy(blk, sl, r).start()
            return carry
        lax.fori_loop(0, rows, issue, 0, unroll=8)

    def w_copies(e, sl):
        return [pltpu.make_async_copy(w1_hbm.at[layer, e], wf1.at[sl], wsem.at[sl, 0]),
                pltpu.make_async_copy(w3_hbm.at[layer, e], wf3.at[sl], wsem.at[sl, 1]),
                pltpu.make_async_copy(w2_hbm.at[layer, e], wf2.at[sl], wsem.at[sl, 2])]

    @pl.when(i == 0)
    def _():
        start_rows(0, 0)
        for cp in w_copies(be_ref[0], 0):
            cp.start()

    @pl.when(i < n)
    def _():
        @pl.when(i + 1 < n)
        def _():
            start_rows(i + 1, 1 - slot)

        @pl.when(first_ref[i] == 1)
        def _():
            p = par_ref[i]
            for cp in w_copies(be_ref[i], p):
                cp.wait()
            w1b[...] = wf1[p].astype(BF16)
            w3b[...] = wf3[p].astype(BF16)
            w2b[...] = wf2[p].astype(BF16)

            @pl.when(nxt_ref[i] >= 0)
            def _():
                for cp in w_copies(nxt_ref[i], 1 - p):
                    cp.start()

        def wait_row(r, carry):
            x_copy(i, slot, r).wait()
            return carry
        lax.fori_loop(0, rows, wait_row, 0, unroll=8)
        x = xbuf[slot].astype(BF16)
        h = jax.nn.silu(jnp.dot(x, w1b[...], preferred_element_type=F32)) * jnp.dot(x, w3b[...], preferred_element_type=F32)
        y = jnp.dot(h.astype(BF16), w2b[...], preferred_element_type=F32)
        o_ref[...] = y * g_ref[...]

    @pl.when(i >= n)
    def _():
        o_ref[...] = jnp.zeros_like(o_ref)


def _moe(h, xn, w_group, b_group, w_expert, b_expert, w1, w3, w2, layer, tm):
    N, D = xn.shape
    G = w_group.shape[1]
    E = w_expert.shape[1]
    EPG = E // G
    FF = w1.shape[3]
    R = MOE_ROWS
    logits = _matmul(xn, jnp.concatenate([w_group, w_expert], 1), tm=tm, name="moe_router")
    tok = jnp.arange(N)
    lg = logits[:, :G] + b_group
    grp = jnp.argmax(lg, -1)
    p_grp = jax.nn.softmax(lg, -1)[tok, grp]
    le = (logits[:, G:] + b_expert).reshape(N, G, EPG)[tok, grp]
    top_v, top_i = lax.top_k(le, MOE_TOPK)
    gates = (p_grp[:, None] * jax.nn.softmax(top_v, -1)).reshape(-1)
    experts = (grp[:, None] * EPG + top_i).reshape(-1).astype(jnp.int32)
    A = N * MOE_TOPK
    ch = _tile(A, 512, LANES)
    e3 = experts.reshape(A // ch, 1, ch)
    chunk_spec = pl.BlockSpec((1, 1, ch), lambda c: (c, 0, 0))
    col_spec = pl.BlockSpec((E, 1), lambda c: (0, 0))
    counts = pl.pallas_call(
        _moe_count_kernel, grid=(A // ch,), in_specs=[chunk_spec], out_specs=col_spec,
        out_shape=jax.ShapeDtypeStruct((E, 1), F32), compiler_params=_cp("arbitrary"), name="moe_count",
    )(e3)[:, 0].astype(jnp.int32)
    pcounts = (counts + R - 1) // R * R
    pends = jnp.cumsum(pcounts)
    pstarts = pends - pcounts
    dest = pl.pallas_call(
        _moe_dest_kernel, grid=(A // ch,), in_specs=[chunk_spec, col_spec], out_specs=chunk_spec,
        out_shape=jax.ShapeDtypeStruct((A // ch, 1, ch), jnp.int32),
        scratch_shapes=[pltpu.VMEM((E, 1), F32)], compiler_params=_cp("arbitrary"), name="moe_dest",
    )(e3, pstarts.astype(F32).reshape(E, 1)).reshape(A)
    n_blocks = -(-A // R) + E
    P = n_blocks * R
    blk = jnp.arange(n_blocks)
    blk_e = jnp.minimum(jnp.sum((pends[None, :] <= (blk * R)[:, None]).astype(jnp.int32), axis=1), E - 1)
    n_used = pends[-1] // R
    first = (jnp.concatenate([jnp.ones((1,), bool), blk_e[1:] != blk_e[:-1]]) & (blk < n_used)).astype(jnp.int32)
    par = (jnp.cumsum(first) - 1) % 2
    first_pos = jnp.where(first == 1, blk, n_blocks)
    nfp = lax.cummin(jnp.concatenate([first_pos[1:], jnp.full((1,), n_blocks)]), reverse=True)
    nxt = jnp.where(nfp < n_blocks, blk_e[jnp.minimum(nfp, n_blocks - 1)], -1)
    a_pad = jnp.full((P,), -1, jnp.int32).at[dest].set(jnp.arange(A, dtype=jnp.int32))
    src = jnp.maximum(a_pad, 0)
    tok_pad = src // MOE_TOPK
    gate_pad = jnp.where(a_pad >= 0, gates[src], 0.0)
    i32 = lambda x: x.astype(jnp.int32)
    grid_spec = pltpu.PrefetchScalarGridSpec(
        num_scalar_prefetch=6,
        grid=(n_blocks,),
        in_specs=[
            pl.BlockSpec(memory_space=pl.ANY),
            pl.BlockSpec((R, 1), lambda i, *_: (i, 0)),
            pl.BlockSpec(memory_space=pl.ANY),
            pl.BlockSpec(memory_space=pl.ANY),
            pl.BlockSpec(memory_space=pl.ANY),
        ],
        out_specs=pl.BlockSpec((R, D), lambda i, *_: (i, 0)),
        scratch_shapes=[pltpu.VMEM((2, R, D), F32),
                        pltpu.VMEM((2, D, FF), F32), pltpu.VMEM((2, D, FF), F32), pltpu.VMEM((2, FF, D), F32),
                        pltpu.VMEM((D, FF), BF16), pltpu.VMEM((D, FF), BF16), pltpu.VMEM((FF, D), BF16),
                        pltpu.SemaphoreType.DMA((2,)), pltpu.SemaphoreType.DMA((2, 3))],
    )
    yb = pl.pallas_call(
        functools.partial(_moe_kernel, layer=layer, rows=R), grid_spec=grid_spec,
        out_shape=jax.ShapeDtypeStruct((P, D), F32),
        compiler_params=_cp("arbitrary"), name="moe_experts",
    )(i32(blk_e), i32(first), i32(nxt), i32(par), i32(n_used).reshape(1), i32(tok_pad),
      xn, gate_pad.reshape(P, 1), w1, w3, w2)
    d2 = dest.reshape(N, MOE_TOPK)
    return h + (yb[d2[:, 0]] + yb[d2[:, 1]])


def kernel(x_prompt, x_sample, state_ret, cache_k_c, cache_v_c, cache_kidx_c, cache_ckv_d, cache_kpe_d, page_table, norm_mix, norm_ffn, ret_w_in, ret_gn, ret_w_o, cm_w_in, cm_ln_g, cm_ln_b, cm_w_s, cm_b_s, cm_w_o, dsa_w_in, dsa_q_norm, dsa_k_norm, dsa_w_o, mla_w_in, mla_qa_norm, mla_kv_norm, mla_w_uq, mla_qn_nope, mla_qn_pe, mla_kpe_norm, mla_w_uk, mla_w_uv, mla_w_o, moe_w_group, moe_b_group, moe_w_expert, moe_b_expert, moe_w1, moe_w3, moe_w2):
    B, L, D = x_prompt.shape
    Bs, T, _ = x_sample.shape
    past = page_table.shape[1] * cache_k_c.shape[2]
    dims = (B, L, Bs, T, past)
    Np, Ns = B * L, Bs * T
    N = Np + Ns
    tm = _tile(N, 512, 16)
    depth = norm_mix.shape[0]
    h = jnp.concatenate([x_prompt.reshape(Np, D), x_sample.reshape(Ns, D)], 0)
    outs = {k: [] for k in ("ret_p", "ret_s", "cm_s", "kc", "vc", "ic", "ckv", "kpe")}
    for i in range(depth):
        kind, j = i % 4, i // 4
        xn = _rmsnorm(h, norm_mix[i], tm)
        if kind == 0:
            proj = _matmul(xn, ret_w_in[j], tm=tm, name="ret_in")
            o, s_p, s_s = _retention(proj, state_ret, j, ret_gn[j], dims)
            outs["ret_p"].append(s_p)
            outs["ret_s"].append(s_s)
            w_o = ret_w_o[j]
        elif kind == 1:
            uv = _matmul(xn, cm_w_in[j], tm=tm, act="gelu", name="cm_in")
            o, vn = _cm_gate(uv, cm_ln_g[j], cm_ln_b[j], cm_w_s[j], cm_b_s[j], dims)
            outs["cm_s"].append(vn.reshape(Bs, T, -1))
            w_o = cm_w_o[j]
        elif kind == 2:
            proj = _matmul(xn, dsa_w_in[j], tm=tm, name="dsa_in")
            o, k, v, ki = _dsa(proj, (cache_k_c, cache_v_c, cache_kidx_c), j, page_table,
                               dsa_q_norm[j], dsa_k_norm[j], dims, dsa_w_o.shape[1] // cache_k_c.shape[-1])
            outs["kc"].append(k)
            outs["vc"].append(v)
            outs["ic"].append(ki)
            w_o = dsa_w_o[j]
        else:
            o, ckv, kpe = _mla(xn, (cache_ckv_d, cache_kpe_d), j, page_table, mla_w_in[j], mla_qa_norm[j],
                               mla_kv_norm[j], mla_w_uq[j], mla_qn_nope[j], mla_qn_pe[j], mla_kpe_norm[j],
                               mla_w_uk[j], mla_w_uv[j], dims, tm)
            outs["ckv"].append(ckv)
            outs["kpe"].append(kpe)
            w_o = mla_w_o[j]
        h = _matmul(o, w_o, tm=tm, residual=h, name="mix_out")
        xn = _rmsnorm(h, norm_ffn[i], tm, out_dtype=F32)
        h = _moe(h, xn, moe_w_group[i], moe_b_group[i], moe_w_expert[i], moe_b_expert[i],
                 moe_w1, moe_w3, moe_w2, i, tm)

    KVH, HD = cache_k_c.shape[3:]
    stack_p = lambda xs, shp: jnp.stack([x[:Np].reshape((B, L) + shp) for x in xs])
    stack_s = lambda xs, shp: jnp.stack([x[Np:].reshape((Bs, T) + shp) for x in xs])
    return (h[:Np].reshape(B, L, D), h[Np:].reshape(Bs, T, D),
            jnp.stack(outs["ret_p"]), jnp.stack(outs["ret_s"]), jnp.stack(outs["cm_s"]),
            stack_p(outs["kc"], (KVH, HD)), stack_p(outs["vc"], (KVH, HD)), stack_p(outs["ic"], (cache_kidx_c.shape[-1],)),
            stack_s(outs["kc"], (KVH, HD)), stack_s(outs["vc"], (KVH, HD)), stack_s(outs["ic"], (cache_kidx_c.shape[-1],)),
            stack_p(outs["ckv"], (cache_ckv_d.shape[-1],)), stack_p(outs["kpe"], (cache_kpe_d.shape[-1],)),
            stack_s(outs["ckv"], (cache_ckv_d.shape[-1],)), stack_s(outs["kpe"], (cache_kpe_d.shape[-1],)))
```

```python
import functools
import math

import jax
import jax.numpy as jnp
from jax import lax
from jax.experimental import pallas as pl
from jax.experimental.pallas import tpu as pltpu

F32 = jnp.float32
BF16 = jnp.bfloat16
EPS = 1e-6
LANES = 128
VMEM_LIMIT = 56 * 1024 * 1024
NEG = -1e30

Q_BLOCK = 128
IDX_TOPK = 256
RET_THETA = 10000.0
DSA_THETA = 500000.0
MLA_THETA = 10000.0
MOE_TOPK = 2
MOE_ROWS = 128
WEIGHT_DMA_PRIORITY = 1


def _cp(*sem):
    return pltpu.CompilerParams(dimension_semantics=sem, vmem_limit_bytes=VMEM_LIMIT)


def _tile(n, pref, mult=8):
    best = None
    for d in range(mult, min(n, pref) + 1, mult):
        if n % d == 0:
            best = d
    assert best is not None, (n, pref, mult)
    return best


def _rmsnorm_kernel(x_ref, g_ref, o_ref):
    x = x_ref[...]
    y = x * lax.rsqrt(jnp.mean(x * x, -1, keepdims=True) + EPS)
    o_ref[...] = (y * g_ref[...]).astype(o_ref.dtype)


def _rmsnorm(h, g, tm, out_dtype=None):
    n, d = h.shape
    out_dtype = out_dtype or BF16
    return pl.pallas_call(
        _rmsnorm_kernel,
        grid=(n // tm,),
        in_specs=[pl.BlockSpec((tm, d), lambda i: (i, 0)), pl.BlockSpec((1, d), lambda i: (0, 0))],
        out_specs=pl.BlockSpec((tm, d), lambda i: (i, 0)),
        out_shape=jax.ShapeDtypeStruct((n, d), out_dtype),
        compiler_params=_cp("parallel"),
        name="rmsnorm",
    )(h, g.reshape(1, d))


def _mm_kernel(*refs, act, has_res):
    if has_res:
        a_ref, w_ref, r_ref, o_ref, wb_ref = refs
    else:
        a_ref, w_ref, o_ref, wb_ref = refs

    @pl.when(pl.program_id(1) == 0)
    def _():
        wb_ref[...] = w_ref[...].astype(BF16)

    acc = jnp.dot(a_ref[...].astype(BF16), wb_ref[...], preferred_element_type=F32)
    if act == "gelu":
        acc = jax.nn.gelu(acc)
    if has_res:
        acc = acc + r_ref[...]
    o_ref[...] = acc.astype(o_ref.dtype)


def _matmul(a, w, *, tm, out_dtype=F32, act=None, residual=None, name="matmul"):
    m, k = a.shape
    n = w.shape[1]
    tn_max = 1024 if k <= 2048 else (512 if k <= 4096 else 256)
    tn = n if n <= tn_max else tn_max
    in_specs = [pl.BlockSpec((tm, k), lambda j, i: (i, 0)), pl.BlockSpec((k, tn), lambda j, i: (0, j))]
    args = [a, w]
    if residual is not None:
        in_specs.append(pl.BlockSpec((tm, tn), lambda j, i: (i, j)))
        args.append(residual)
    return pl.pallas_call(
        functools.partial(_mm_kernel, act=act, has_res=residual is not None),
        grid=(pl.cdiv(n, tn), m // tm),
        in_specs=in_specs,
        out_specs=pl.BlockSpec((tm, tn), lambda j, i: (i, j)),
        out_shape=jax.ShapeDtypeStruct((m, n), out_dtype),
        scratch_shapes=[pltpu.VMEM((k, tn), BF16)],
        compiler_params=_cp("parallel", "arbitrary"),
        name=name,
    )(*args)


def _head_mm_kernel(a_ref, w_ref, o_ref, wb_ref, *, trans_w):
    @pl.when(pl.program_id(1) == 0)
    def _():
        wb_ref[...] = w_ref[...].astype(BF16)

    a = a_ref[...].astype(BF16)
    if trans_w:
        acc = lax.dot_general(a, wb_ref[...], (((1,), (1,)), ((), ())), preferred_element_type=F32)
    else:
        acc = jnp.dot(a, wb_ref[...], preferred_element_type=F32)
    o_ref[...] = acc.astype(o_ref.dtype)


def _head_mm(a, w, heads, *, trans_w, tm, out_dtype, name):
    m = a.shape[0]
    ka = a.shape[1] // heads
    if trans_w:
        nw = w.shape[0]
        wblk = (nw, ka)
    else:
        nw = w.shape[1] // heads
        wblk = (ka, nw)
    return pl.pallas_call(
        functools.partial(_head_mm_kernel, trans_w=trans_w),
        grid=(heads, m // tm),
        in_specs=[pl.BlockSpec((tm, ka), lambda h, i: (i, h)), pl.BlockSpec(wblk, lambda h, i: (0, h))],
        out_specs=pl.BlockSpec((tm, nw), lambda h, i: (i, h)),
        out_shape=jax.ShapeDtypeStruct((m, heads * nw), out_dtype),
        scratch_shapes=[pltpu.VMEM(wblk, BF16)],
        compiler_params=_cp("parallel", "arbitrary"),
        name=name,
    )(a, w)


def _rope_tables_full(pos, theta, n_rot):
    half = n_rot // 2
    inv = theta ** (-jnp.arange(half, dtype=F32) / half)
    ang = pos.astype(F32)[:, None] * inv[None, :]
    return jnp.cos(ang), jnp.sin(ang)


def _rope_tables_packed(pos, theta, n_rot, group):
    half = n_rot // 2
    cos, sin = _rope_tables_full(pos, theta, n_rot)
    lane = jnp.arange(LANES)
    within = lane % group
    fidx = within % half
    c = jnp.where(within < n_rot, cos[:, fidx], 1.0)
    s1 = jnp.where(within < half, -sin[:, fidx], 0.0)
    s2 = jnp.where((within >= half) & (within < n_rot), sin[:, fidx], 0.0)
    return c.astype(F32), s1.astype(F32), s2.astype(F32)


def _rope_packed(y, c, s1, s2, half):
    return y * c + pltpu.roll(y, LANES - half, 1) * s1 + pltpu.roll(y, half, 1) * s2


def _rope_split(x, cos, sin):
    half = cos.shape[-1]
    x1, x2 = x[:, :half], x[:, half:]
    return jnp.concatenate([x1 * cos - x2 * sin, x2 * cos + x1 * sin], -1)


def _groupnorm_gate(o, gn, g):
    mu = jnp.mean(o, -1, keepdims=True)
    d = o - mu
    var = jnp.mean(d * d, -1, keepdims=True)
    return (d * lax.rsqrt(var + EPS) * gn) * jax.nn.silu(g)


def _ret_prompt_kernel(q_ref, k_ref, v_ref, g_ref, cos_ref, sin_ref, din_ref, dq_ref, dk_ref, dch_ref,
                       gn_ref, o_ref, s_ref, *, scale):
    @pl.when(pl.program_id(2) == 0)
    def _():
        s_ref[...] = jnp.zeros_like(s_ref)

    cos, sin = cos_ref[...], sin_ref[...]
    q = _rope_split(q_ref[...], cos, sin)
    k = _rope_split(k_ref[...], cos, sin) * scale
    qb, kb, vb = q.astype(BF16), k.astype(BF16), v_ref[...].astype(BF16)
    state = s_ref[0, 0]
    att = lax.dot_general(qb, kb, (((1,), (1,)), ((), ())), preferred_element_type=F32) * din_ref[0]
    o = jnp.dot(att.astype(BF16), vb, preferred_element_type=F32)
    o = o + jnp.dot(qb, state.astype(BF16), preferred_element_type=F32) * dq_ref[0]
    kdt = (k * dk_ref[0]).T.astype(BF16)
    s_ref[0, 0] = state * dch_ref[0] + jnp.dot(kdt, vb, preferred_element_type=F32)
    o_ref[...] = _groupnorm_gate(o, gn_ref[0], g_ref[...]).astype(o_ref.dtype)


def _ret_sample_kernel(q_ref, k_ref, v_ref, g_ref, cos_ref, sin_ref, din_ref, dq_ref, dk_ref, dch_ref,
                       gn_ref, s0_ref, o_ref, s_ref, *, scale, nb, t):
    rows = nb * t
    cos, sin = cos_ref[...], sin_ref[...]
    q = _rope_split(q_ref[...], cos, sin)
    k = _rope_split(k_ref[...], cos, sin) * scale
    qb, kb, vb = q.astype(BF16), k.astype(BF16), v_ref[...].astype(BF16)
    att = lax.dot_general(qb, kb, (((1,), (1,)), ((), ())), preferred_element_type=F32) * din_ref[0]
    o = jnp.dot(att.astype(BF16), vb, preferred_element_type=F32)
    kd = k * dk_ref[0]
    pad = LANES - rows
    kdt = jnp.concatenate([kd, jnp.zeros((pad, kd.shape[1]), F32)], 0).T
    vpad = jnp.concatenate([vb, jnp.zeros((pad, vb.shape[1]), BF16)], 0)
    rowb = lax.broadcasted_iota(jnp.int32, (rows, 1), 0) // t
    colb = lax.broadcasted_iota(jnp.int32, (1, LANES), 1) // t
    dq = dq_ref[0]
    dch = dch_ref[0]
    for i in range(nb):
        state = s0_ref[0, i, 0]
        cross = jnp.dot(qb, state.astype(BF16), preferred_element_type=F32) * dq
        o = o + jnp.where(rowb == i, cross, 0.0)
        kdt_i = jnp.where(colb == i, kdt, 0.0).astype(BF16)
        s_ref[i, 0] = state * dch + jnp.dot(kdt_i, vpad, preferred_element_type=F32)
    o_ref[...] = _groupnorm_gate(o, gn_ref[0], g_ref[...]).astype(o_ref.dtype)


def _retention(proj, state_s, layer, gn, dims):
    B, L, Bs, T, past = dims
    Np, Ns = B * L, Bs * T
    H, DK, DV = state_s.shape[2:]
    half = DK // 2
    scale = DK ** -0.5
    vblk0 = (2 * H * DK) // DV
    log_gamma = jnp.log1p(-(2.0 ** (-5.0 - jnp.arange(H, dtype=F32))))

    def decay(C):
        idx = jnp.arange(C, dtype=F32)
        rel = idx[:, None] - idx[None, :]
        d_inner = jnp.where(rel >= 0, jnp.exp(log_gamma[:, None, None] * jnp.maximum(rel, 0.0)), 0.0)
        d_query = jnp.exp((idx[None, :] + 1.0) * log_gamma[:, None])[:, :, None]
        d_key = jnp.exp((C - 1.0 - idx[None, :]) * log_gamma[:, None])[:, :, None]
        d_chunk = jnp.broadcast_to(jnp.exp(C * log_gamma)[:, None, None], (H, 1, DV))
        return d_inner, d_query, d_key, d_chunk

    gn3 = gn.reshape(H, 1, DV)
    C = 128 if L % 128 == 0 else L
    n = L // C
    cos, sin = _rope_tables_full(jnp.arange(L), RET_THETA, DK)
    d_inner, d_query, d_key, d_chunk = decay(C)
    row = lambda b, h, c: b * n + c
    o_p, s_p = pl.pallas_call(
        functools.partial(_ret_prompt_kernel, scale=scale),
        grid=(B, H, n),
        in_specs=[
            pl.BlockSpec((C, DK), lambda b, h, c: (row(b, h, c), h)),
            pl.BlockSpec((C, DK), lambda b, h, c: (row(b, h, c), H + h)),
            pl.BlockSpec((C, DV), lambda b, h, c: (row(b, h, c), vblk0 + h)),
            pl.BlockSpec((C, DV), lambda b, h, c: (row(b, h, c), vblk0 + H + h)),
            pl.BlockSpec((C, half), lambda b, h, c: (c, 0)),
            pl.BlockSpec((C, half), lambda b, h, c: (c, 0)),
            pl.BlockSpec((1, C, C), lambda b, h, c: (h, 0, 0)),
            pl.BlockSpec((1, C, 1), lambda b, h, c: (h, 0, 0)),
            pl.BlockSpec((1, C, 1), lambda b, h, c: (h, 0, 0)),
            pl.BlockSpec((1, 1, DV), lambda b, h, c: (h, 0, 0)),
            pl.BlockSpec((1, 1, DV), lambda b, h, c: (h, 0, 0)),
        ],
        out_specs=[
            pl.BlockSpec((C, DV), lambda b, h, c: (row(b, h, c), h)),
            pl.BlockSpec((1, 1, DK, DV), lambda b, h, c: (b, h, 0, 0)),
        ],
        out_shape=[jax.ShapeDtypeStruct((Np, H * DV), BF16), jax.ShapeDtypeStruct((B, H, DK, DV), F32)],
        compiler_params=_cp("parallel", "parallel", "arbitrary"),
        name="retention_prompt",
    )(proj, proj, proj, proj, cos, sin, d_inner, d_query, d_key, d_chunk, gn3)

    nb = _tile(Bs, 8, 1)
    while (nb * T) % 8 or Np % (nb * T):
        nb -= 1
    rows = nb * T
    cos_s, sin_s = _rope_tables_full(past + jnp.arange(T), RET_THETA, DK)
    cos_s, sin_s = jnp.tile(cos_s, (nb, 1)), jnp.tile(sin_s, (nb, 1))
    di, dqs, dks, dchs = decay(T)
    same = (jnp.arange(rows)[:, None] // T) == (jnp.arange(rows)[None, :] // T)
    di = jnp.where(same[None], jnp.tile(di, (1, nb, nb)), 0.0)
    dqs, dks = jnp.tile(dqs, (1, nb, 1)), jnp.tile(dks, (1, nb, 1))
    r0 = Np // rows
    o_s, s_s = pl.pallas_call(
        functools.partial(_ret_sample_kernel, scale=scale, nb=nb, t=T),
        grid=(Bs // nb, H),
        in_specs=[
            pl.BlockSpec((rows, DK), lambda b, h: (r0 + b, h)),
            pl.BlockSpec((rows, DK), lambda b, h: (r0 + b, H + h)),
            pl.BlockSpec((rows, DV), lambda b, h: (r0 + b, vblk0 + h)),
            pl.BlockSpec((rows, DV), lambda b, h: (r0 + b, vblk0 + H + h)),
            pl.BlockSpec((rows, half), lambda b, h: (0, 0)),
            pl.BlockSpec((rows, half), lambda b, h: (0, 0)),
            pl.BlockSpec((1, rows, rows), lambda b, h: (h, 0, 0)),
            pl.BlockSpec((1, rows, 1), lambda b, h: (h, 0, 0)),
            pl.BlockSpec((1, rows, 1), lambda b, h: (h, 0, 0)),
            pl.BlockSpec((1, 1, DV), lambda b, h: (h, 0, 0)),
            pl.BlockSpec((1, 1, DV), lambda b, h: (h, 0, 0)),
            pl.BlockSpec((1, nb, 1, DK, DV), lambda b, h: (layer, b, h, 0, 0)),
        ],
        out_specs=[
            pl.BlockSpec((rows, DV), lambda b, h: (b, h)),
            pl.BlockSpec((nb, 1, DK, DV), lambda b, h: (b, h, 0, 0)),
        ],
        out_shape=[jax.ShapeDtypeStruct((Ns, H * DV), BF16), jax.ShapeDtypeStruct((Bs, H, DK, DV), F32)],
        compiler_params=_cp("parallel", "parallel"),
        name="retention_sample",
    )(proj, proj, proj, proj, cos_s, sin_s, di, dqs, dks, dchs, gn3, state_s)
    return jnp.concatenate([o_p, o_s], 0), s_p, s_s


def _cm_gate_kernel(u_ref, v_ref, lg_ref, lb_ref, ws_ref, bs_ref, y_ref, vn_ref, *, groups):
    v = v_ref[...]
    mu = jnp.mean(v, -1, keepdims=True)
    d = v - mu
    var = jnp.mean(d * d, -1, keepdims=True)
    vn = d * lax.rsqrt(var + EPS) * lg_ref[...] + lb_ref[...]
    vn_ref[...] = vn
    cw = v.shape[1] // groups
    for g in range(groups):
        sl = slice(g * cw, (g + 1) * cw)
        s = jnp.dot(ws_ref[0, g].astype(BF16), vn[:, sl].astype(BF16), preferred_element_type=F32) + bs_ref[0, g]
        y_ref[:, sl] = (u_ref[:, sl] * s).astype(y_ref.dtype)


def _cm_gate(uv, ln_g, ln_b, w_s, b_s, dims):
    B, L, Bs, T, past = dims
    Np, Ns = B * L, Bs * T
    W = uv.shape[1] // 2
    G, C = w_s.shape[0], w_s.shape[1]
    assert L % C == 0 and Ns % C == 0 and C % T == 0 and T <= C
    npc = Np // C
    r = jnp.arange(C)
    tril = r[:, None] >= r[None, :]
    ws_p = jnp.where(tril[None], w_s, 0.0)
    tt = r % T
    same = (r[:, None] // T) == (r[None, :] // T)
    ws_s = jnp.where((same & (tt[:, None] >= tt[None, :]))[None], w_s[:, tt[:, None], tt[None, :]], 0.0)
    ws2 = jnp.stack([ws_p, ws_s])
    bs2 = jnp.stack([b_s, b_s[:, tt]])[..., None]
    kind = lambda i: jnp.where(i < npc, 0, 1)
    y, vn = pl.pallas_call(
        functools.partial(_cm_gate_kernel, groups=G),
        grid=((Np + Ns) // C,),
        in_specs=[
            pl.BlockSpec((C, W), lambda i: (i, 0)),
            pl.BlockSpec((C, W), lambda i: (i, 1)),
            pl.BlockSpec((1, W), lambda i: (0, 0)),
            pl.BlockSpec((1, W), lambda i: (0, 0)),
            pl.BlockSpec((1, G, C, C), lambda i: (kind(i), 0, 0, 0)),
            pl.BlockSpec((1, G, C, 1), lambda i: (kind(i), 0, 0, 0)),
        ],
        out_specs=[
            pl.BlockSpec((C, W), lambda i: (i, 0)),
            pl.BlockSpec((C, W), lambda i: (jnp.maximum(i - npc, 0), 0)),
        ],
        out_shape=[jax.ShapeDtypeStruct((Np + Ns, W), BF16), jax.ShapeDtypeStruct((Ns, W), F32)],
        compiler_params=_cp("arbitrary"),
        name="cm_gate",
    )(uv, uv, ln_g.reshape(1, W), ln_b.reshape(1, W), ws2, bs2)
    return y, vn


def _causal_widths(L, tq):
    if L % (4 * tq) == 0:
        return tuple(L * k // 4 for k in range(1, 5))
    return (L,)


def _for_causal_width(fn, need, widths):
    lo = 0
    for W in widths:
        @pl.when((need > lo) & (need <= W))
        def _(W=W):
            fn(W)
        lo = W


def _topk_mask(sc, valid, k, idx_bits):
    int_min = jnp.int32(-2 ** 31)
    kf = jnp.float32(k)
    bits = pltpu.bitcast(sc + 0.0, jnp.int32)
    key = jnp.where(bits < 0, bits ^ jnp.int32(0x7FFFFFFF), bits)
    key = jnp.where(valid, key, int_min)

    def count(m):
        return jnp.sum(m.astype(F32), axis=1, keepdims=True)

    t0 = jnp.where(count(key >= 0) >= kf, jnp.int32(0), int_min)

    def value_bit(i, t):
        cand = t | (jnp.int32(1) << (30 - i))
        return jnp.where(count(key >= cand) >= kf, cand, t)

    thr = lax.fori_loop(0, 31, value_bit, t0)
    gt = key > thr
    eq = key == thr
    need = kf - count(gt)
    idx = lax.broadcasted_iota(jnp.int32, sc.shape, 1)

    def index_bit(i, x):
        cand = x | (jnp.int32(1) << (idx_bits - 1 - i))
        return jnp.where(count(eq & (idx < cand)) < need, cand, x)

    last = lax.fori_loop(0, idx_bits, index_bit, jnp.zeros_like(thr))
    return valid & (gt | (eq & (idx <= last)))


def _dsa_prep_kernel(x_ref, qn_ref, kn_ref, c_ref, s1_ref, s2_ref, ci_ref, si1_ref, si2_ref,
                     q_ref, k_ref, kb_ref, v_ref, vb_ref, qi_ref, ki_ref, kib_ref, wh_ref,
                     *, heads, kv_heads, idx_heads, half, half_i, wh_scale):
    hd = LANES
    c, s1, s2 = c_ref[...], s1_ref[...], s2_ref[...]
    ci, si1, si2 = ci_ref[...], si1_ref[...], si2_ref[...]

    def normed(x, gain):
        return x * lax.rsqrt(jnp.mean(x * x, -1, keepdims=True) + EPS) * gain

    off = 0
    for h in range(heads):
        y = _rope_packed(normed(x_ref[:, off:off + hd], qn_ref[...]), c, s1, s2, half)
        q_ref[:, h * hd:(h + 1) * hd] = y.astype(q_ref.dtype)
        off += hd
    for g in range(kv_heads):
        y = _rope_packed(normed(x_ref[:, off:off + hd], kn_ref[...]), c, s1, s2, half)
        k_ref[:, g * hd:(g + 1) * hd] = y
        kb_ref[:, g * hd:(g + 1) * hd] = y.astype(BF16)
        off += hd
    v = x_ref[:, off:off + kv_heads * hd]
    v_ref[...] = v
    vb_ref[...] = v.astype(BF16)
    off += kv_heads * hd
    for h in range(idx_heads):
        y = _rope_packed(x_ref[:, off:off + hd], ci, si1, si2, half_i)
        qi_ref[:, h * hd:(h + 1) * hd] = y.astype(qi_ref.dtype)
        off += hd
    y = _rope_packed(x_ref[:, off:off + hd], ci, si1, si2, half_i)
    ki_ref[...] = y
    kib_ref[...] = y.astype(BF16)
    off += hd
    wh_ref[...] = x_ref[:, off:off + idx_heads] * wh_scale


def _dsa_prompt_kernel(q_ref, qi_ref, wh_ref, k_ref, v_ref, ki_ref, o_ref, bias_ref,
                       *, kv_heads, rep, idx_heads, k_top, idx_scale, scale, widths):
    hd = LANES
    tq = bias_ref.shape[0]
    dn = (((1,), (1,)), ((), ()))

    def attend(W):
        kib = ki_ref[0:W, :]
        wh = wh_ref[...] * idx_scale
        sc = jnp.zeros((tq, W), F32)
        for h in range(idx_heads):
            d = lax.dot_general(qi_ref[:, h * hd:(h + 1) * hd], kib, dn, preferred_element_type=F32)
            sc = sc + wh[:, h:h + 1] * jnp.maximum(d, 0.0)
        qpos = pl.program_id(1) * tq + lax.broadcasted_iota(jnp.int32, (tq, W), 0)
        spos = lax.broadcasted_iota(jnp.int32, (tq, W), 1)
        mask = _topk_mask(sc, spos <= qpos, k_top, max(1, (W - 1).bit_length()))
        bias_ref[:, 0:W] = jnp.where(mask, 0.0, -jnp.inf)
        for g in range(kv_heads):
            kg = k_ref[0:W, g * hd:(g + 1) * hd]
            vg = v_ref[0:W, g * hd:(g + 1) * hd]
            for r in range(rep):
                hq = g * rep + r
                s = lax.dot_general(q_ref[:, hq * hd:(hq + 1) * hd], kg, dn,
                                    preferred_element_type=F32) * scale + bias_ref[:, 0:W]
                e = jnp.exp(s - jnp.max(s, -1, keepdims=True))
                p = (e * (1.0 / jnp.sum(e, -1, keepdims=True))).astype(BF16)
                o_ref[:, hq * hd:(hq + 1) * hd] = jnp.dot(p, vg, preferred_element_type=F32).astype(o_ref.dtype)

    _for_causal_width(attend, (pl.program_id(1) + 1) * tq, widths)


def _paged_kernel(body, layer, planes, n_dense, n_out, pc, n_pages, page_rows):
    nchunks = n_pages // pc
    n_paged = len(planes)

    def kernel(pt_ref, *refs):
        paged = refs[:n_paged]
        dense = refs[n_paged:n_paged + n_dense]
        outs = refs[n_paged + n_dense:n_paged + n_dense + n_out]
        rest = refs[n_paged + n_dense + n_out:]
        bufs, sem, user = rest[:n_paged], rest[n_paged], rest[n_paged + 1:]
        c = pl.program_id(1)
        step = pl.program_id(0) * nchunks + c
        total = pl.num_programs(0) * nchunks
        slot = step % 2

        def copies(st, sl):
            base = (st // nchunks) * n_pages + (st % nchunks) * pc
            out = []
            for a in range(n_paged):
                for p in range(pc):
                    page = pt_ref[base + p]
                    rows = pl.ds(p * page_rows, page_rows)
                    if planes[a] is None:
                        out.append(pltpu.make_async_copy(paged[a].at[layer, page], bufs[a].at[sl, rows], sem.at[sl, a]))
                    elif planes[a] == "T":
                        out.append(pltpu.make_async_copy(paged[a].at[layer, page], bufs[a].at[sl, :, rows], sem.at[sl, a]))
                    else:
                        for g in range(planes[a]):
                            out.append(pltpu.make_async_copy(paged[a].at[layer, page, :, g],
                                                             bufs[a].at[sl, g, rows], sem.at[sl, a]))
            return out

        @pl.when(step == 0)
        def _():
            for cp in copies(step, slot):
                cp.start()

        @pl.when(step + 1 < total)
        def _():
            for cp in copies(step + 1, 1 - slot):
                cp.start()

        for cp in copies(step, slot):
            cp.wait()
        body(c, nchunks, [bufs[a].at[slot] for a in range(n_paged)], dense, outs, user)

    return kernel


def _paged_call(body, page_table, layer, paged, dense, dense_specs, out_shapes, out_specs, user_scratch, pc, name,
                transposed=()):
    Bs, n_pages = page_table.shape
    page_rows = paged[0].shape[2]
    nchunks = n_pages // pc
    planes = ["T" if i in transposed else (a.shape[3] if a.ndim == 5 else None) for i, a in enumerate(paged)]

    def window(a, g):
        if g is None:
            return (2, pc * page_rows, a.shape[3])
        if g == "T":
            return (2, a.shape[2], pc * page_rows)
        return (2, g, pc * page_rows, a.shape[4])

    kernel = _paged_kernel(body, layer, planes, len(dense), 1, pc, n_pages, page_rows)
    scratch = [pltpu.VMEM(window(a, g), a.dtype) for a, g in zip(paged, planes)]
    scratch.append(pltpu.SemaphoreType.DMA((2, len(paged))))
    scratch.extend(user_scratch)
    grid_spec = pltpu.PrefetchScalarGridSpec(
        num_scalar_prefetch=1,
        grid=(Bs, nchunks),
        in_specs=[pl.BlockSpec(memory_space=pl.ANY)] * len(paged) + list(dense_specs),
        out_specs=out_specs,
        scratch_shapes=scratch,
    )
    return pl.pallas_call(
        kernel, grid_spec=grid_spec, out_shape=out_shapes,
        compiler_params=_cp("arbitrary", "arbitrary"), name=name,
    )(page_table.reshape(-1), *paged, *dense)


def _dsa_sample_select_body(c, nchunks, bufs, dense, outs, user, *, t, idx_heads, k_top, idx_bits, idx_scale):
    (ki_buf,) = bufs
    qi_ref, wh_ref, kin_ref = dense
    (bias_ref,) = outs
    (sc_ref,) = user
    w = ki_buf.shape[0]
    past = nchunks * w

    def scores(keys_bf16):
        d = lax.dot_general(qi_ref[0], keys_bf16, (((1,), (1,)), ((), ())), preferred_element_type=F32)
        wd = jnp.maximum(d, 0.0) * (wh_ref[0] * idx_scale)
        return jnp.sum(wd.reshape(t, idx_heads, wd.shape[1]), axis=1)

    group, slab, S = sc_ref.shape
    gi = pl.program_id(0) % group

    @pl.when((c == 0) & (gi == 0))
    def _():
        sc_ref[...] = jnp.zeros_like(sc_ref)

    sc_ref[gi, 0:t, pl.ds(pl.multiple_of(c * w, LANES), w)] = scores(ki_buf[...].astype(BF16))

    @pl.when(c == nchunks - 1)
    def _():
        sc_ref[gi, 0:t, past:past + LANES] = scores(kin_ref[0])

    @pl.when((c == nchunks - 1) & (gi == group - 1))
    def _():
        rows = group * slab
        row = lax.broadcasted_iota(jnp.int32, (rows, S), 0) % slab
        col = lax.broadcasted_iota(jnp.int32, (rows, S), 1)
        valid = (col < past) | ((col - past <= row) & (col - past < t))
        mask = _topk_mask(sc_ref[...].reshape(rows, S), valid, k_top, idx_bits)
        bias_ref[...] = jnp.where(mask, 0.0, NEG).reshape(group, slab, S)[:, 0:t]


def _expand_rows(x, reps):
    t, w = x.shape
    row = lax.broadcasted_iota(jnp.int32, (t * reps, w), 0) // reps
    out = jnp.broadcast_to(x[0:1], (t * reps, w))
    for i in range(1, t):
        out = jnp.where(row == i, jnp.broadcast_to(x[i:i + 1], (t * reps, w)), out)
    return out


def _softmax_update(s, v_bf16, m_ref, l_ref, acc_ref, idx):
    m_old = m_ref[idx]
    m_new = jnp.maximum(m_old, jnp.max(s, -1, keepdims=True))
    a = jnp.exp(m_old - m_new)
    p = jnp.exp(s - m_new)
    l_ref[idx] = l_ref[idx] * a + jnp.sum(p, -1, keepdims=True)
    acc_ref[idx] = acc_ref[idx] * a + jnp.dot(p.astype(BF16), v_bf16, preferred_element_type=F32)
    m_ref[idx] = m_new


def _dsa_sample_attend_body(c, nchunks, bufs, dense, outs, user, *, kv_heads, rep, scale):
    k_buf, v_buf = bufs
    q_ref, bias_ref, tail_ref, kn_ref, vn_ref = dense
    (o_ref,) = outs
    m_ref, l_ref, acc_ref = user
    hd = LANES

    @pl.when(c == 0)
    def _():
        m_ref[...] = jnp.full_like(m_ref, NEG)
        l_ref[...] = jnp.zeros_like(l_ref)
        acc_ref[...] = jnp.zeros_like(acc_ref)

    def attend(keys, vals, bias):
        bias = _expand_rows(bias, rep)
        for g in range(kv_heads):
            s = lax.dot_general(q_ref[0, g], keys(g), (((1,), (1,)), ((), ())),
                                preferred_element_type=F32) * scale + bias
            _softmax_update(s, vals(g), m_ref, l_ref, acc_ref, g)

    attend(lambda g: k_buf[g].astype(BF16), lambda g: v_buf[g].astype(BF16), bias_ref[0])

    @pl.when(c == nchunks - 1)
    def _():
        attend(lambda g: kn_ref[0, :, g * hd:(g + 1) * hd], lambda g: vn_ref[0, :, g * hd:(g + 1) * hd], tail_ref[0])
        o_ref[0] = (acc_ref[...] / l_ref[...]).astype(o_ref.dtype)


def _dsa(proj, caches, layer, page_table, q_norm, k_norm, dims, dsa_heads):
    cache_k, cache_v, cache_ki = caches
    B, L, Bs, T, past = dims
    Np, Ns = B * L, Bs * T
    N = Np + Ns
    _, n_pool, page, KVH, HD = cache_k.shape
    IDX_DIM = cache_ki.shape[-1]
    H = dsa_heads
    REP = H // KVH
    IDXH = (proj.shape[1] - H * HD - 2 * KVH * HD - IDX_DIM) // (IDX_DIM + 1)
    assert HD == LANES and IDX_DIM == LANES
    rope_dims, idx_rope = HD // 4, IDX_DIM // 4
    idx_scale = IDX_DIM ** -0.5
    scale = HD ** -0.5

    pos = jnp.concatenate([jnp.arange(L), jnp.tile(past + jnp.arange(T), Bs)])
    tabs = _rope_tables_packed(pos, DSA_THETA, rope_dims, HD) + _rope_tables_packed(pos, DSA_THETA, idx_rope, IDX_DIM)
    tm = _tile(math.gcd(L, Ns), 256)
    lpt, npt = L // tm, Np // tm
    tab_spec = pl.BlockSpec((tm, LANES), lambda i: (jnp.where(i < npt, i % lpt, lpt + i - npt), 0))
    rowspec = lambda w: pl.BlockSpec((tm, w), lambda i: (i, 0))
    shapes = [(H * HD, BF16), (KVH * HD, F32), (KVH * HD, BF16), (KVH * HD, F32), (KVH * HD, BF16),
              (IDXH * IDX_DIM, BF16), (IDX_DIM, F32), (IDX_DIM, BF16), (IDXH, F32)]
    q, k, kb, v, vb, qi, ki, kib, wh = pl.pallas_call(
        functools.partial(_dsa_prep_kernel, heads=H, kv_heads=KVH, idx_heads=IDXH, half=rope_dims // 2,
                          half_i=idx_rope // 2, wh_scale=IDXH ** -0.5),
        grid=(N // tm,),
        in_specs=[rowspec(proj.shape[1]), pl.BlockSpec((1, HD), lambda i: (0, 0)),
                  pl.BlockSpec((1, HD), lambda i: (0, 0))] + [tab_spec] * 6,
        out_specs=[rowspec(w) for w, _ in shapes],
        out_shape=[jax.ShapeDtypeStruct((N, w), dt) for w, dt in shapes],
        compiler_params=_cp("parallel"),
        name="dsa_prep",
    )(proj, q_norm.reshape(1, HD), k_norm.reshape(1, HD), *tabs)

    tq = Q_BLOCK
    nq = L // tq
    k_top = min(IDX_TOPK, L // 4)
    o_p = pl.pallas_call(
        functools.partial(_dsa_prompt_kernel, kv_heads=KVH, rep=REP, idx_heads=IDXH, k_top=k_top,
                          idx_scale=idx_scale, scale=scale, widths=_causal_widths(L, tq)),
        grid=(B, nq),
        in_specs=[
            pl.BlockSpec((tq, H * HD), lambda b, i: (b * nq + i, 0)),
            pl.BlockSpec((tq, IDXH * IDX_DIM), lambda b, i: (b * nq + i, 0)),
            pl.BlockSpec((tq, IDXH), lambda b, i: (b * nq + i, 0)),
            pl.BlockSpec((L, KVH * HD), lambda b, i: (b, 0)),
            pl.BlockSpec((L, KVH * HD), lambda b, i: (b, 0)),
            pl.BlockSpec((L, IDX_DIM), lambda b, i: (b, 0)),
        ],
        out_specs=pl.BlockSpec((tq, H * HD), lambda b, i: (b * nq + i, 0)),
        out_shape=jax.ShapeDtypeStruct((Np, H * HD), BF16),
        scratch_shapes=[pltpu.VMEM((tq, L), F32)],
        compiler_params=_cp("parallel", "arbitrary"),
        name="dsa_prompt",
    )(q, qi, wh, kb, vb, kib)

    n_pages = page_table.shape[1]
    pc = _tile(n_pages, 16, 1)
    w = pc * page
    S = past + LANES
    k_top_s = min(IDX_TOPK, (past + T) // 4)
    rows = 8
    group = _tile(Bs, 8, 1)
    assert T <= rows
    qi_s = qi[Np:].reshape(Bs, T * IDXH, IDX_DIM)
    wh_s = wh[Np:].reshape(Bs, T * IDXH, 1)
    pad_new = lambda x: jnp.pad(x[Np:].reshape(Bs, T, x.shape[1]), ((0, 0), (0, LANES - T), (0, 0)))
    bias = _paged_call(
        functools.partial(_dsa_sample_select_body, t=T, idx_heads=IDXH, k_top=k_top_s,
                          idx_bits=(S - 1).bit_length(), idx_scale=idx_scale),
        page_table, layer, [cache_ki],
        [qi_s, wh_s, pad_new(kib)],
        [pl.BlockSpec((1, T * IDXH, IDX_DIM), lambda b, c, pt: (b, 0, 0)),
         pl.BlockSpec((1, T * IDXH, 1), lambda b, c, pt: (b, 0, 0)),
         pl.BlockSpec((1, LANES, IDX_DIM), lambda b, c, pt: (b, 0, 0))],
        jax.ShapeDtypeStruct((Bs, T, S), F32),
        pl.BlockSpec((group, T, S), lambda b, c, pt: (b // group, 0, 0)),
        [pltpu.VMEM((group, rows, S), F32)],
        pc, "dsa_sample_select")

    q_s = q[Np:].reshape(Bs, T, KVH, REP, HD).transpose(0, 2, 1, 3, 4).reshape(Bs, KVH, T * REP, HD)
    o_s = _paged_call(
        functools.partial(_dsa_sample_attend_body, kv_heads=KVH, rep=REP, scale=scale),
        page_table, layer, [cache_k, cache_v],
        [q_s, bias, bias, pad_new(kb), pad_new(vb)],
        [pl.BlockSpec((1, KVH, T * REP, HD), lambda b, c, pt: (b, 0, 0, 0)),
         pl.BlockSpec((1, T, w), lambda b, c, pt: (b, 0, c)),
         pl.BlockSpec((1, T, LANES), lambda b, c, pt: (b, 0, past // LANES)),
         pl.BlockSpec((1, LANES, KVH * HD), lambda b, c, pt: (b, 0, 0)),
         pl.BlockSpec((1, LANES, KVH * HD), lambda b, c, pt: (b, 0, 0))],
        jax.ShapeDtypeStruct((Bs, KVH, T * REP, HD), BF16),
        pl.BlockSpec((1, KVH, T * REP, HD), lambda b, c, pt: (b, 0, 0, 0)),
        [pltpu.VMEM((KVH, T * REP, 1), F32), pltpu.VMEM((KVH, T * REP, 1), F32),
         pltpu.VMEM((KVH, T * REP, HD), F32)],
        pc, "dsa_sample_attend")
    o_s = o_s.reshape(Bs, KVH, T, REP, HD).transpose(0, 2, 1, 3, 4).reshape(Ns, H * HD)
    return jnp.concatenate([o_p, o_s], 0), k, v, ki


def _norm_rope_pairs(x, gain, c, s1, s2, group, half):
    lane = lax.broadcasted_iota(jnp.int32, x.shape, 1)
    x2 = x * x
    inv = jnp.zeros_like(x)
    for j in range(LANES // group):
        sel = (lane >= j * group) & (lane < (j + 1) * group)
        ms = jnp.sum(jnp.where(sel, x2, 0.0), -1, keepdims=True) * (1.0 / group)
        inv = jnp.where(sel, lax.rsqrt(ms + EPS), inv)
    return _rope_packed(x * inv * gain, c, s1, s2, half)


def _mla_prep_kernel(x_ref, qa_ref, kvn_ref, kpn_ref, c_ref, s1_ref, s2_ref,
                     cq_ref, ckv_ref, ckvb_ref, kpe_ref, kpeb_ref, *, q_lora, kv_lora, rope):
    def normed(x, gain):
        return x * lax.rsqrt(jnp.mean(x * x, -1, keepdims=True) + EPS) * gain

    cq_ref[...] = normed(x_ref[:, 0:q_lora], qa_ref[...]).astype(cq_ref.dtype)
    ckv = normed(x_ref[:, q_lora:q_lora + kv_lora], kvn_ref[...])
    ckv_ref[...] = ckv
    ckvb_ref[...] = ckv.astype(BF16)
    y = _norm_rope_pairs(x_ref[:, q_lora + kv_lora:q_lora + kv_lora + LANES], kpn_ref[...],
                         c_ref[...], s1_ref[...], s2_ref[...], rope, rope // 2)
    kpe_ref[...] = y[:, 0:rope]
    kpeb_ref[...] = y[:, 0:rope].astype(BF16)


def _mla_q_kernel(x_ref, gn_ref, gp_ref, c_ref, s1_ref, s2_ref, qn_ref, qp_ref, *, heads, nope, rope):
    def normed(x, gain):
        return x * lax.rsqrt(jnp.mean(x * x, -1, keepdims=True) + EPS) * gain

    for h in range(heads):
        qn_ref[:, h * nope:(h + 1) * nope] = normed(x_ref[:, h * nope:(h + 1) * nope], gn_ref[...]).astype(qn_ref.dtype)
    base = heads * nope
    c, s1, s2 = c_ref[...], s1_ref[...], s2_ref[...]
    for j in range(heads * rope // LANES):
        y = _norm_rope_pairs(x_ref[:, base + j * LANES:base + (j + 1) * LANES], gp_ref[...], c, s1, s2, rope, rope // 2)
        qp_ref[:, j * LANES:(j + 1) * LANES] = y.astype(qp_ref.dtype)


def _mla_prompt_kernel(qn_ref, qp_ref, kn_ref, kp_ref, v_ref, o_ref, *, heads, nope, rope, vdim, scale, widths):
    tq = qn_ref.shape[0]
    dn = (((1,), (1,)), ((), ()))
    per_tile = LANES // rope

    def attend(W):
        qpos = pl.program_id(1) * tq + lax.broadcasted_iota(jnp.int32, (tq, W), 0)
        spos = lax.broadcasted_iota(jnp.int32, (tq, W), 1)
        causal = spos <= qpos
        kp = kp_ref[0:W, :]
        lane = lax.broadcasted_iota(jnp.int32, (tq, LANES), 1)
        for h in range(heads):
            j, sub = h // per_tile, h % per_tile
            qp = qp_ref[:, j * LANES:(j + 1) * LANES]
            qp = jnp.where((lane >= sub * rope) & (lane < (sub + 1) * rope), qp, jnp.zeros_like(qp))
            s = (lax.dot_general(qn_ref[:, h * nope:(h + 1) * nope], kn_ref[0:W, h * nope:(h + 1) * nope], dn,
                                 preferred_element_type=F32)
                 + lax.dot_general(qp, kp, dn, preferred_element_type=F32)) * scale
            s = jnp.where(causal, s, -jnp.inf)
            e = jnp.exp(s - jnp.max(s, -1, keepdims=True))
            p = (e * (1.0 / jnp.sum(e, -1, keepdims=True))).astype(BF16)
            o_ref[:, h * vdim:(h + 1) * vdim] = jnp.dot(p, v_ref[0:W, h * vdim:(h + 1) * vdim],
                                                        preferred_element_type=F32).astype(o_ref.dtype)

    _for_causal_width(attend, (pl.program_id(1) + 1) * tq, widths)


def _mla_sample_body(c, nchunks, bufs, dense, outs, user, *, t, heads, scale):
    c_buf, r_buf = bufs
    ql_ref, qp_ref, cn_ref, rn_ref = dense
    (o_ref,) = outs
    m_ref, l_ref, acc_ref = user
    dn = (((1,), (1,)), ((), ()))

    @pl.when(c == 0)
    def _():
        m_ref[...] = jnp.full_like(m_ref, NEG)
        l_ref[...] = jnp.zeros_like(l_ref)
        acc_ref[...] = jnp.zeros_like(acc_ref)

    lat = c_buf[...].astype(BF16)
    s = (lax.dot_general(ql_ref[0], lat, dn, preferred_element_type=F32)
         + jnp.dot(qp_ref[0], r_buf[...].astype(BF16), preferred_element_type=F32)) * scale
    _softmax_update(s, lat, m_ref, l_ref, acc_ref, 0)

    @pl.when(c == nchunks - 1)
    def _():
        lat_n = cn_ref[0]
        s = (lax.dot_general(ql_ref[0], lat_n, dn, preferred_element_type=F32)
             + lax.dot_general(qp_ref[0], rn_ref[0], dn, preferred_element_type=F32)) * scale
        trow = lax.broadcasted_iota(jnp.int32, s.shape, 0) // heads
        col = lax.broadcasted_iota(jnp.int32, s.shape, 1)
        s = jnp.where((col <= trow) & (col < t), s, NEG)
        _softmax_update(s, lat_n, m_ref, l_ref, acc_ref, 0)
        o_ref[0] = (acc_ref[0] / l_ref[0]).astype(o_ref.dtype)


def _mla(xn, caches, layer, page_table, w_in, qa_norm, kv_norm, w_uq, qn_nope, qn_pe, kpe_norm, w_uk, w_uv, dims, tm):
    cache_ckv, cache_kpe = caches
    B, L, Bs, T, past = dims
    Np, Ns = B * L, Bs * T
    N = Np + Ns
    KV = cache_ckv.shape[-1]
    R = cache_kpe.shape[-1]
    QL = w_uq.shape[0]
    H, NOPE = w_uk.shape[1], w_uk.shape[2]
    VD = w_uv.shape[2]
    assert NOPE == LANES and LANES % R == 0 and (H * R) % LANES == 0 and (QL + KV) % LANES == 0
    scale = (NOPE + R) ** -0.5

    w_in_p = jnp.pad(w_in, ((0, 0), (0, LANES - R)))
    proj = _matmul(xn, w_in_p, tm=tm, name="mla_in")
    pos = jnp.concatenate([jnp.arange(L), jnp.tile(past + jnp.arange(T), Bs)])
    tabs = _rope_tables_packed(pos, MLA_THETA, R, R)
    tp = _tile(math.gcd(L, Ns), 256)
    lpt, npt = L // tp, Np // tp
    tab_spec = pl.BlockSpec((tp, LANES), lambda i: (jnp.where(i < npt, i % lpt, lpt + i - npt), 0))
    rowspec = lambda w: pl.BlockSpec((tp, w), lambda i: (i, 0))
    vec = lambda w: pl.BlockSpec((1, w), lambda i: (0, 0))
    tile_gain = lambda g: jnp.tile(g, LANES // R).reshape(1, LANES)
    shapes = [(QL, BF16), (KV, F32), (KV, BF16), (R, F32), (R, BF16)]
    cq, ckv, ckvb, kpe, kpeb = pl.pallas_call(
        functools.partial(_mla_prep_kernel, q_lora=QL, kv_lora=KV, rope=R),
        grid=(N // tp,),
        in_specs=[rowspec(proj.shape[1]), vec(QL), vec(KV), vec(LANES)] + [tab_spec] * 3,
        out_specs=[rowspec(w) for w, _ in shapes],
        out_shape=[jax.ShapeDtypeStruct((N, w), dt) for w, dt in shapes],
        compiler_params=_cp("parallel"),
        name="mla_prep",
    )(proj, qa_norm.reshape(1, QL), kv_norm.reshape(1, KV), tile_gain(kpe_norm), *tabs)

    w3 = w_uq.reshape(QL, H, NOPE + R)
    w_uq_p = jnp.concatenate([w3[:, :, :NOPE].reshape(QL, H * NOPE), w3[:, :, NOPE:].reshape(QL, H * R)], 1)
    qraw = _matmul(cq, w_uq_p, tm=tm, name="mla_uq")
    q_nope, q_pe = pl.pallas_call(
        functools.partial(_mla_q_kernel, heads=H, nope=NOPE, rope=R),
        grid=(N // tp,),
        in_specs=[rowspec(qraw.shape[1]), vec(NOPE), vec(LANES)] + [tab_spec] * 3,
        out_specs=[rowspec(H * NOPE), rowspec(H * R)],
        out_shape=[jax.ShapeDtypeStruct((N, H * NOPE), BF16), jax.ShapeDtypeStruct((N, H * R), BF16)],
        compiler_params=_cp("parallel"),
        name="mla_q",
    )(qraw, qn_nope.reshape(1, NOPE), tile_gain(qn_pe), *tabs)

    tk = _tile(Np, 512)
    k_nope = _matmul(ckvb[:Np], w_uk.reshape(KV, H * NOPE), tm=tk, out_dtype=BF16, name="mla_uk")
    v = _matmul(ckvb[:Np], w_uv.reshape(KV, H * VD), tm=tk, out_dtype=BF16, name="mla_uv")
    tq = Q_BLOCK
    nq = L // tq
    o_p = pl.pallas_call(
        functools.partial(_mla_prompt_kernel, heads=H, nope=NOPE, rope=R, vdim=VD, scale=scale,
                          widths=_causal_widths(L, tq)),
        grid=(B, nq),
        in_specs=[
            pl.BlockSpec((tq, H * NOPE), lambda b, i: (b * nq + i, 0)),
            pl.BlockSpec((tq, H * R), lambda b, i: (b * nq + i, 0)),
            pl.BlockSpec((L, H * NOPE), lambda b, i: (b, 0)),
            pl.BlockSpec((L, LANES), lambda b, i: (b, 0)),
            pl.BlockSpec((L, H * VD), lambda b, i: (b, 0)),
        ],
        out_specs=pl.BlockSpec((tq, H * VD), lambda b, i: (b * nq + i, 0)),
        out_shape=jax.ShapeDtypeStruct((Np, H * VD), BF16),
        compiler_params=_cp("parallel", "arbitrary"),
        name="mla_prompt",
    )(q_nope, q_pe, k_nope, jnp.tile(kpeb[:Np], (1, LANES // R)), v)

    ts = _tile(Ns, 512)
    q_lat = _head_mm(q_nope[Np:], w_uk.reshape(KV, H * NOPE), H, trans_w=True, tm=ts, out_dtype=BF16,
                     name="mla_absorb_q")
    n_pages = page_table.shape[1]
    pc = _tile(n_pages, 16, 1)
    pad_new = lambda x: jnp.pad(x[Np:].reshape(Bs, T, x.shape[1]), ((0, 0), (0, LANES - T), (0, 0)))
    o_lat = _paged_call(
        functools.partial(_mla_sample_body, t=T, heads=H, scale=scale),
        page_table, layer, [cache_ckv, jnp.swapaxes(cache_kpe, 2, 3)],
        [q_lat.reshape(Bs, T * H, KV), q_pe[Np:].reshape(Bs, T * H, R), pad_new(ckvb), pad_new(kpeb)],
        [pl.BlockSpec((1, T * H, KV), lambda b, c, pt: (b, 0, 0)),
         pl.BlockSpec((1, T * H, R), lambda b, c, pt: (b, 0, 0)),
         pl.BlockSpec((1, LANES, KV), lambda b, c, pt: (b, 0, 0)),
         pl.BlockSpec((1, LANES, R), lambda b, c, pt: (b, 0, 0))],
        jax.ShapeDtypeStruct((Bs, T * H, KV), BF16),
        pl.BlockSpec((1, T * H, KV), lambda b, c, pt: (b, 0, 0)),
        [pltpu.VMEM((1, T * H, 1), F32), pltpu.VMEM((1, T * H, 1), F32), pltpu.VMEM((1, T * H, KV), F32)],
        pc, "mla_sample", transposed=(1,))
    o_s = _head_mm(o_lat.reshape(Ns, H * KV), w_uv.reshape(KV, H * VD), H, trans_w=False, tm=ts,
                   out_dtype=BF16, name="mla_absorb_o")
    return jnp.concatenate([o_p, o_s], 0), ckv, kpe


def _moe_count_kernel(e_ref, cnt_ref):
    @pl.when(pl.program_id(0) == 0)
    def _():
        cnt_ref[...] = jnp.zeros_like(cnt_ref)

    e = e_ref[0]
    hit = lax.broadcasted_iota(jnp.int32, (cnt_ref.shape[0], e.shape[1]), 0) == e
    cnt_ref[...] += jnp.sum(hit.astype(F32), axis=1, keepdims=True)


def _moe_dest_kernel(e_ref, pstart_ref, dest_ref, run_ref):
    @pl.when(pl.program_id(0) == 0)
    def _():
        run_ref[...] = pstart_ref[...]

    e = e_ref[0]
    ch = e.shape[1]
    hit = lax.broadcasted_iota(jnp.int32, (run_ref.shape[0], ch), 0) == e
    tri = (lax.broadcasted_iota(jnp.int32, (ch, ch), 0) <= lax.broadcasted_iota(jnp.int32, (ch, ch), 1))
    prefix = jnp.dot(hit.astype(BF16), tri.astype(BF16), preferred_element_type=F32)
    pos = prefix - 1.0 + run_ref[...]
    dest_ref[0] = jnp.sum(jnp.where(hit, pos, 0.0), axis=0, keepdims=True).astype(jnp.int32)
    run_ref[...] += jnp.sum(hit.astype(F32), axis=1, keepdims=True)


def _moe_kernel(be_ref, first_ref, nxt_ref, par_ref, nblk_ref, tok_ref, x_hbm, g_ref, w1_hbm, w3_hbm, w2_hbm,
                o_ref, xbuf, wf1, wf3, wf2, w1b, w3b, w2b, xsem, wsem, *, layer, rows):
    i = pl.program_id(0)
    n = nblk_ref[0]
    slot = i % 2

    def x_copy(blk, sl, r):
        return pltpu.make_async_copy(x_hbm.at[pl.ds(tok_ref[blk * rows + r], 1)], xbuf.at[sl, pl.ds(r, 1)], xsem.at[sl])

    def start_rows(blk, sl):
        def issue(r, carry):
            x_copy(blk, sl, r).start()
            return carry
        lax.fori_loop(0, rows, issue, 0, unroll=8)

    def w_copies(e, sl):
        return [pltpu.make_async_copy(w1_hbm.at[layer, e], wf1.at[sl], wsem.at[sl, 0]),
                pltpu.make_async_copy(w3_hbm.at[layer, e], wf3.at[sl], wsem.at[sl, 1]),
                pltpu.make_async_copy(w2_hbm.at[layer, e], wf2.at[sl], wsem.at[sl, 2])]

    @pl.when(i == 0)
    def _():
        start_rows(0, 0)
        for cp in w_copies(be_ref[0], 0):
            cp.start(priority=WEIGHT_DMA_PRIORITY)

    @pl.when(i < n)
    def _():
        @pl.when(i + 1 < n)
        def _():
            start_rows(i + 1, 1 - slot)

        @pl.when(first_ref[i] == 1)
        def _():
            p = par_ref[i]
            for cp in w_copies(be_ref[i], p):
                cp.wait()
            w1b[...] = wf1[p].astype(BF16)
            w3b[...] = wf3[p].astype(BF16)
            w2b[...] = wf2[p].astype(BF16)

            @pl.when(nxt_ref[i] >= 0)
            def _():
                for cp in w_copies(nxt_ref[i], 1 - p):
                    cp.start(priority=WEIGHT_DMA_PRIORITY)

        def wait_row(r, carry):
            x_copy(i, slot, r).wait()
            return carry
        lax.fori_loop(0, rows, wait_row, 0, unroll=8)
        x = xbuf[slot].astype(BF16)
        h = jax.nn.silu(jnp.dot(x, w1b[...], preferred_element_type=F32)) * jnp.dot(x, w3b[...], preferred_element_type=F32)
        y = jnp.dot(h.astype(BF16), w2b[...], preferred_element_type=F32)
        o_ref[...] = y * g_ref[...]

    @pl.when(i >= n)
    def _():
        o_ref[...] = jnp.zeros_like(o_ref)


def _moe(h, xn, w_group, b_group, w_expert, b_expert, w1, w3, w2, layer, tm):
    N, D = xn.shape
    G = w_group.shape[1]
    E = w_expert.shape[1]
    EPG = E // G
    FF = w1.shape[3]
    R = MOE_ROWS
    logits = _matmul(xn, jnp.concatenate([w_group, w_expert], 1), tm=tm, name="moe_router")
    tok = jnp.arange(N)
    lg = logits[:, :G] + b_group
    grp = jnp.argmax(lg, -1)
    p_grp = jax.nn.softmax(lg, -1)[tok, grp]
    le = (logits[:, G:] + b_expert).reshape(N, G, EPG)[tok, grp]
    top_v, top_i = lax.top_k(le, MOE_TOPK)
    gates = (p_grp[:, None] * jax.nn.softmax(top_v, -1)).reshape(-1)
    experts = (grp[:, None] * EPG + top_i).reshape(-1).astype(jnp.int32)
    A = N * MOE_TOPK
    ch = _tile(A, 512, LANES)
    e3 = experts.reshape(A // ch, 1, ch)
    chunk_spec = pl.BlockSpec((1, 1, ch), lambda c: (c, 0, 0))
    col_spec = pl.BlockSpec((E, 1), lambda c: (0, 0))
    counts = pl.pallas_call(
        _moe_count_kernel, grid=(A // ch,), in_specs=[chunk_spec], out_specs=col_spec,
        out_shape=jax.ShapeDtypeStruct((E, 1), F32), compiler_params=_cp("arbitrary"), name="moe_count",
    )(e3)[:, 0].astype(jnp.int32)
    pcounts = (counts + R - 1) // R * R
    pends = jnp.cumsum(pcounts)
    pstarts = pends - pcounts
    dest = pl.pallas_call(
        _moe_dest_kernel, grid=(A // ch,), in_specs=[chunk_spec, col_spec], out_specs=chunk_spec,
        out_shape=jax.ShapeDtypeStruct((A // ch, 1, ch), jnp.int32),
        scratch_shapes=[pltpu.VMEM((E, 1), F32)], compiler_params=_cp("arbitrary"), name="moe_dest",
    )(e3, pstarts.astype(F32).reshape(E, 1)).reshape(A)
    n_blocks = -(-A // R) + E
    P = n_blocks * R
    blk = jnp.arange(n_blocks)
    blk_e = jnp.minimum(jnp.sum((pends[None, :] <= (blk * R)[:, None]).astype(jnp.int32), axis=1), E - 1)
    n_used = pends[-1] // R
    first = (jnp.concatenate([jnp.ones((1,), bool), blk_e[1:] != blk_e[:-1]]) & (blk < n_used)).astype(jnp.int32)
    par = (jnp.cumsum(first) - 1) % 2
    first_pos = jnp.where(first == 1, blk, n_blocks)
    nfp = lax.cummin(jnp.concatenate([first_pos[1:], jnp.full((1,), n_blocks)]), reverse=True)
    nxt = jnp.where(nfp < n_blocks, blk_e[jnp.minimum(nfp, n_blocks - 1)], -1)
    a_pad = jnp.full((P,), -1, jnp.int32).at[dest].set(jnp.arange(A, dtype=jnp.int32))
    src = jnp.maximum(a_pad, 0)
    tok_pad = src // MOE_TOPK
    gate_pad = jnp.where(a_pad >= 0, gates[src], 0.0)
    i32 = lambda x: x.astype(jnp.int32)
    grid_spec = pltpu.PrefetchScalarGridSpec(
        num_scalar_prefetch=6,
        grid=(n_blocks,),
        in_specs=[
            pl.BlockSpec(memory_space=pl.ANY),
            pl.BlockSpec((R, 1), lambda i, *_: (i, 0)),
            pl.BlockSpec(memory_space=pl.ANY),
            pl.BlockSpec(memory_space=pl.ANY),
            pl.BlockSpec(memory_space=pl.ANY),
        ],
        out_specs=pl.BlockSpec((R, D), lambda i, *_: (i, 0)),
        scratch_shapes=[pltpu.VMEM((2, R, D), F32),
                        pltpu.VMEM((2, D, FF), F32), pltpu.VMEM((2, D, FF), F32), pltpu.VMEM((2, FF, D), F32),
                        pltpu.VMEM((D, FF), BF16), pltpu.VMEM((D, FF), BF16), pltpu.VMEM((FF, D), BF16),
                        pltpu.SemaphoreType.DMA((2,)), pltpu.SemaphoreType.DMA((2, 3))],
    )
    yb = pl.pallas_call(
        functools.partial(_moe_kernel, layer=layer, rows=R), grid_spec=grid_spec,
        out_shape=jax.ShapeDtypeStruct((P, D), F32),
        compiler_params=_cp("arbitrary"), name="moe_experts",
    )(i32(blk_e), i32(first), i32(nxt), i32(par), i32(n_used).reshape(1), i32(tok_pad),
      xn, gate_pad.reshape(P, 1), w1, w3, w2)
    d2 = dest.reshape(N, MOE_TOPK)
    return h + (yb[d2[:, 0]] + yb[d2[:, 1]])


def kernel(x_prompt, x_sample, state_ret, cache_k_c, cache_v_c, cache_kidx_c, cache_ckv_d, cache_kpe_d, page_table, norm_mix, norm_ffn, ret_w_in, ret_gn, ret_w_o, cm_w_in, cm_ln_g, cm_ln_b, cm_w_s, cm_b_s, cm_w_o, dsa_w_in, dsa_q_norm, dsa_k_norm, dsa_w_o, mla_w_in, mla_qa_norm, mla_kv_norm, mla_w_uq, mla_qn_nope, mla_qn_pe, mla_kpe_norm, mla_w_uk, mla_w_uv, mla_w_o, moe_w_group, moe_b_group, moe_w_expert, moe_b_expert, moe_w1, moe_w3, moe_w2):
    B, L, D = x_prompt.shape
    Bs, T, _ = x_sample.shape
    past = page_table.shape[1] * cache_k_c.shape[2]
    dims = (B, L, Bs, T, past)
    Np, Ns = B * L, Bs * T
    N = Np + Ns
    tm = _tile(N, 512, 16)
    depth = norm_mix.shape[0]
    h = jnp.concatenate([x_prompt.reshape(Np, D), x_sample.reshape(Ns, D)], 0)
    outs = {k: [] for k in ("ret_p", "ret_s", "cm_s", "kc", "vc", "ic", "ckv", "kpe")}
    for i in range(depth):
        kind, j = i % 4, i // 4
        xn = _rmsnorm(h, norm_mix[i], tm)
        if kind == 0:
            proj = _matmul(xn, ret_w_in[j], tm=tm, name="ret_in")
            o, s_p, s_s = _retention(proj, state_ret, j, ret_gn[j], dims)
            outs["ret_p"].append(s_p)
            outs["ret_s"].append(s_s)
            w_o = ret_w_o[j]
        elif kind == 1:
            uv = _matmul(xn, cm_w_in[j], tm=tm, act="gelu", name="cm_in")
            o, vn = _cm_gate(uv, cm_ln_g[j], cm_ln_b[j], cm_w_s[j], cm_b_s[j], dims)
            outs["cm_s"].append(vn.reshape(Bs, T, -1))
            w_o = cm_w_o[j]
        elif kind == 2:
            proj = _matmul(xn, dsa_w_in[j], tm=tm, name="dsa_in")
            o, k, v, ki = _dsa(proj, (cache_k_c, cache_v_c, cache_kidx_c), j, page_table,
                               dsa_q_norm[j], dsa_k_norm[j], dims, dsa_w_o.shape[1] // cache_k_c.shape[-1])
            outs["kc"].append(k)
            outs["vc"].append(v)
            outs["ic"].append(ki)
            w_o = dsa_w_o[j]
        else:
            o, ckv, kpe = _mla(xn, (cache_ckv_d, cache_kpe_d), j, page_table, mla_w_in[j], mla_qa_norm[j],
                               mla_kv_norm[j], mla_w_uq[j], mla_qn_nope[j], mla_qn_pe[j], mla_kpe_norm[j],
                               mla_w_uk[j], mla_w_uv[j], dims, tm)
            outs["ckv"].append(ckv)
            outs["kpe"].append(kpe)
            w_o = mla_w_o[j]
        h = _matmul(o, w_o, tm=tm, residual=h, name="mix_out")
        xn = _rmsnorm(h, norm_ffn[i], tm, out_dtype=F32)
        h = _moe(h, xn, moe_w_group[i], moe_b_group[i], moe_w_expert[i], moe_b_expert[i],
                 moe_w1, moe_w3, moe_w2, i, tm)

    KVH, HD = cache_k_c.shape[3:]
    stack_p = lambda xs, shp: jnp.stack([x[:Np].reshape((B, L) + shp) for x in xs])
    stack_s = lambda xs, shp: jnp.stack([x[Np:].reshape((Bs, T) + shp) for x in xs])
    return (h[:Np].reshape(B, L, D), h[Np:].reshape(Bs, T, D),
            jnp.stack(outs["ret_p"]), jnp.stack(outs["ret_s"]), jnp.stack(outs["cm_s"]),
            stack_p(outs["kc"], (KVH, HD)), stack_p(outs["vc"], (KVH, HD)), stack_p(outs["ic"], (cache_kidx_c.shape[-1],)),
            stack_s(outs["kc"], (KVH, HD)), stack_s(outs["vc"], (KVH, HD)), stack_s(outs["ic"], (cache_kidx_c.shape[-1],)),
            stack_p(outs["ckv"], (cache_ckv_d.shape[-1],)), stack_p(outs["kpe"], (cache_kpe_d.shape[-1],)),
            stack_s(outs["ckv"], (cache_ckv_d.shape[-1],)), stack_s(outs["kpe"], (cache_kpe_d.shape[-1],)))
```

```python
import functools
import math

import jax
import jax.numpy as jnp
from jax import lax
from jax.experimental import pallas as pl
from jax.experimental.pallas import tpu as pltpu

F32 = jnp.float32
BF16 = jnp.bfloat16
EPS = 1e-6
LANES = 128
VMEM_LIMIT = 56 * 1024 * 1024
NEG = -1e30

Q_BLOCK = 128
IDX_TOPK = 256
RET_THETA = 10000.0
DSA_THETA = 500000.0
MLA_THETA = 10000.0
MOE_TOPK = 2
MOE_ROWS = 128
ROW_SLOTS = 3
WEIGHT_DMA_PRIORITY = 1


def _cp(*sem):
    return pltpu.CompilerParams(dimension_semantics=sem, vmem_limit_bytes=VMEM_LIMIT)


def _tile(n, pref, mult=8):
    best = None
    for d in range(mult, min(n, pref) + 1, mult):
        if n % d == 0:
            best = d
    assert best is not None, (n, pref, mult)
    return best


def _rmsnorm_kernel(x_ref, g_ref, o_ref):
    x = x_ref[...]
    y = x * lax.rsqrt(jnp.mean(x * x, -1, keepdims=True) + EPS)
    o_ref[...] = (y * g_ref[...]).astype(o_ref.dtype)


def _rmsnorm(h, g, tm, out_dtype=None):
    n, d = h.shape
    out_dtype = out_dtype or BF16
    return pl.pallas_call(
        _rmsnorm_kernel,
        grid=(n // tm,),
        in_specs=[pl.BlockSpec((tm, d), lambda i: (i, 0)), pl.BlockSpec((1, d), lambda i: (0, 0))],
        out_specs=pl.BlockSpec((tm, d), lambda i: (i, 0)),
        out_shape=jax.ShapeDtypeStruct((n, d), out_dtype),
        compiler_params=_cp("parallel"),
        name="rmsnorm",
    )(h, g.reshape(1, d))


def _mm_kernel(*refs, act, has_res):
    if has_res:
        a_ref, w_ref, r_ref, o_ref, wb_ref = refs
    else:
        a_ref, w_ref, o_ref, wb_ref = refs

    @pl.when(pl.program_id(1) == 0)
    def _():
        wb_ref[...] = w_ref[...].astype(BF16)

    acc = jnp.dot(a_ref[...].astype(BF16), wb_ref[...], preferred_element_type=F32)
    if act == "gelu":
        acc = jax.nn.gelu(acc)
    if has_res:
        acc = acc + r_ref[...]
    o_ref[...] = acc.astype(o_ref.dtype)


def _matmul(a, w, *, tm, out_dtype=F32, act=None, residual=None, name="matmul"):
    m, k = a.shape
    n = w.shape[1]
    tn_max = 1024 if k <= 2048 else (512 if k <= 4096 else 256)
    tn = n if n <= tn_max else tn_max
    in_specs = [pl.BlockSpec((tm, k), lambda j, i: (i, 0)), pl.BlockSpec((k, tn), lambda j, i: (0, j))]
    args = [a, w]
    if residual is not None:
        in_specs.append(pl.BlockSpec((tm, tn), lambda j, i: (i, j)))
        args.append(residual)
    return pl.pallas_call(
        functools.partial(_mm_kernel, act=act, has_res=residual is not None),
        grid=(pl.cdiv(n, tn), m // tm),
        in_specs=in_specs,
        out_specs=pl.BlockSpec((tm, tn), lambda j, i: (i, j)),
        out_shape=jax.ShapeDtypeStruct((m, n), out_dtype),
        scratch_shapes=[pltpu.VMEM((k, tn), BF16)],
        compiler_params=_cp("parallel", "arbitrary"),
        name=name,
    )(*args)


def _head_mm_kernel(a_ref, w_ref, o_ref, wb_ref, *, trans_w):
    @pl.when(pl.program_id(1) == 0)
    def _():
        wb_ref[...] = w_ref[...].astype(BF16)

    a = a_ref[...].astype(BF16)
    if trans_w:
        acc = lax.dot_general(a, wb_ref[...], (((1,), (1,)), ((), ())), preferred_element_type=F32)
    else:
        acc = jnp.dot(a, wb_ref[...], preferred_element_type=F32)
    o_ref[...] = acc.astype(o_ref.dtype)


def _head_mm(a, w, heads, *, trans_w, tm, out_dtype, name):
    m = a.shape[0]
    ka = a.shape[1] // heads
    if trans_w:
        nw = w.shape[0]
        wblk = (nw, ka)
    else:
        nw = w.shape[1] // heads
        wblk = (ka, nw)
    return pl.pallas_call(
        functools.partial(_head_mm_kernel, trans_w=trans_w),
        grid=(heads, m // tm),
        in_specs=[pl.BlockSpec((tm, ka), lambda h, i: (i, h)), pl.BlockSpec(wblk, lambda h, i: (0, h))],
        out_specs=pl.BlockSpec((tm, nw), lambda h, i: (i, h)),
        out_shape=jax.ShapeDtypeStruct((m, heads * nw), out_dtype),
        scratch_shapes=[pltpu.VMEM(wblk, BF16)],
        compiler_params=_cp("parallel", "arbitrary"),
        name=name,
    )(a, w)


def _rope_tables_full(pos, theta, n_rot):
    half = n_rot // 2
    inv = theta ** (-jnp.arange(half, dtype=F32) / half)
    ang = pos.astype(F32)[:, None] * inv[None, :]
    return jnp.cos(ang), jnp.sin(ang)


def _rope_tables_packed(pos, theta, n_rot, group):
    half = n_rot // 2
    cos, sin = _rope_tables_full(pos, theta, n_rot)
    lane = jnp.arange(LANES)
    within = lane % group
    fidx = within % half
    c = jnp.where(within < n_rot, cos[:, fidx], 1.0)
    s1 = jnp.where(within < half, -sin[:, fidx], 0.0)
    s2 = jnp.where((within >= half) & (within < n_rot), sin[:, fidx], 0.0)
    return c.astype(F32), s1.astype(F32), s2.astype(F32)


def _rope_packed(y, c, s1, s2, half):
    return y * c + pltpu.roll(y, LANES - half, 1) * s1 + pltpu.roll(y, half, 1) * s2


def _rope_split(x, cos, sin):
    half = cos.shape[-1]
    x1, x2 = x[:, :half], x[:, half:]
    return jnp.concatenate([x1 * cos - x2 * sin, x2 * cos + x1 * sin], -1)


def _groupnorm_gate(o, gn, g):
    mu = jnp.mean(o, -1, keepdims=True)
    d = o - mu
    var = jnp.mean(d * d, -1, keepdims=True)
    return (d * lax.rsqrt(var + EPS) * gn) * jax.nn.silu(g)


def _ret_prompt_kernel(q_ref, k_ref, v_ref, g_ref, cos_ref, sin_ref, din_ref, dq_ref, dk_ref, dch_ref,
                       gn_ref, o_ref, s_ref, *, scale):
    @pl.when(pl.program_id(2) == 0)
    def _():
        s_ref[...] = jnp.zeros_like(s_ref)

    cos, sin = cos_ref[...], sin_ref[...]
    q = _rope_split(q_ref[...], cos, sin)
    k = _rope_split(k_ref[...], cos, sin) * scale
    qb, kb, vb = q.astype(BF16), k.astype(BF16), v_ref[...].astype(BF16)
    state = s_ref[0, 0]
    att = lax.dot_general(qb, kb, (((1,), (1,)), ((), ())), preferred_element_type=F32) * din_ref[0]
    o = jnp.dot(att.astype(BF16), vb, preferred_element_type=F32)
    o = o + jnp.dot(qb, state.astype(BF16), preferred_element_type=F32) * dq_ref[0]
    kdt = (k * dk_ref[0]).T.astype(BF16)
    s_ref[0, 0] = state * dch_ref[0] + jnp.dot(kdt, vb, preferred_element_type=F32)
    o_ref[...] = _groupnorm_gate(o, gn_ref[0], g_ref[...]).astype(o_ref.dtype)


def _ret_sample_kernel(q_ref, k_ref, v_ref, g_ref, cos_ref, sin_ref, din_ref, dq_ref, dk_ref, dch_ref,
                       gn_ref, s0_ref, o_ref, s_ref, *, scale, nb, t):
    rows = nb * t
    cos, sin = cos_ref[...], sin_ref[...]
    q = _rope_split(q_ref[...], cos, sin)
    k = _rope_split(k_ref[...], cos, sin) * scale
    qb, kb, vb = q.astype(BF16), k.astype(BF16), v_ref[...].astype(BF16)
    att = lax.dot_general(qb, kb, (((1,), (1,)), ((), ())), preferred_element_type=F32) * din_ref[0]
    o = jnp.dot(att.astype(BF16), vb, preferred_element_type=F32)
    kd = k * dk_ref[0]
    pad = LANES - rows
    kdt = jnp.concatenate([kd, jnp.zeros((pad, kd.shape[1]), F32)], 0).T
    vpad = jnp.concatenate([vb, jnp.zeros((pad, vb.shape[1]), BF16)], 0)
    rowb = lax.broadcasted_iota(jnp.int32, (rows, 1), 0) // t
    colb = lax.broadcasted_iota(jnp.int32, (1, LANES), 1) // t
    dq = dq_ref[0]
    dch = dch_ref[0]
    for i in range(nb):
        state = s0_ref[0, i, 0]
        cross = jnp.dot(qb, state.astype(BF16), preferred_element_type=F32) * dq
        o = o + jnp.where(rowb == i, cross, 0.0)
        kdt_i = jnp.where(colb == i, kdt, 0.0).astype(BF16)
        s_ref[i, 0] = state * dch + jnp.dot(kdt_i, vpad, preferred_element_type=F32)
    o_ref[...] = _groupnorm_gate(o, gn_ref[0], g_ref[...]).astype(o_ref.dtype)


def _retention(proj, state_s, layer, gn, dims):
    B, L, Bs, T, past = dims
    Np, Ns = B * L, Bs * T
    H, DK, DV = state_s.shape[2:]
    half = DK // 2
    scale = DK ** -0.5
    vblk0 = (2 * H * DK) // DV
    log_gamma = jnp.log1p(-(2.0 ** (-5.0 - jnp.arange(H, dtype=F32))))

    def decay(C):
        idx = jnp.arange(C, dtype=F32)
        rel = idx[:, None] - idx[None, :]
        d_inner = jnp.where(rel >= 0, jnp.exp(log_gamma[:, None, None] * jnp.maximum(rel, 0.0)), 0.0)
        d_query = jnp.exp((idx[None, :] + 1.0) * log_gamma[:, None])[:, :, None]
        d_key = jnp.exp((C - 1.0 - idx[None, :]) * log_gamma[:, None])[:, :, None]
        d_chunk = jnp.broadcast_to(jnp.exp(C * log_gamma)[:, None, None], (H, 1, DV))
        return d_inner, d_query, d_key, d_chunk

    gn3 = gn.reshape(H, 1, DV)
    C = 128 if L % 128 == 0 else L
    n = L // C
    cos, sin = _rope_tables_full(jnp.arange(L), RET_THETA, DK)
    d_inner, d_query, d_key, d_chunk = decay(C)
    row = lambda b, h, c: b * n + c
    o_p, s_p = pl.pallas_call(
        functools.partial(_ret_prompt_kernel, scale=scale),
        grid=(B, H, n),
        in_specs=[
            pl.BlockSpec((C, DK), lambda b, h, c: (row(b, h, c), h)),
            pl.BlockSpec((C, DK), lambda b, h, c: (row(b, h, c), H + h)),
            pl.BlockSpec((C, DV), lambda b, h, c: (row(b, h, c), vblk0 + h)),
            pl.BlockSpec((C, DV), lambda b, h, c: (row(b, h, c), vblk0 + H + h)),
            pl.BlockSpec((C, half), lambda b, h, c: (c, 0)),
            pl.BlockSpec((C, half), lambda b, h, c: (c, 0)),
            pl.BlockSpec((1, C, C), lambda b, h, c: (h, 0, 0)),
            pl.BlockSpec((1, C, 1), lambda b, h, c: (h, 0, 0)),
            pl.BlockSpec((1, C, 1), lambda b, h, c: (h, 0, 0)),
            pl.BlockSpec((1, 1, DV), lambda b, h, c: (h, 0, 0)),
            pl.BlockSpec((1, 1, DV), lambda b, h, c: (h, 0, 0)),
        ],
        out_specs=[
            pl.BlockSpec((C, DV), lambda b, h, c: (row(b, h, c), h)),
            pl.BlockSpec((1, 1, DK, DV), lambda b, h, c: (b, h, 0, 0)),
        ],
        out_shape=[jax.ShapeDtypeStruct((Np, H * DV), BF16), jax.ShapeDtypeStruct((B, H, DK, DV), F32)],
        compiler_params=_cp("parallel", "parallel", "arbitrary"),
        name="retention_prompt",
    )(proj, proj, proj, proj, cos, sin, d_inner, d_query, d_key, d_chunk, gn3)

    nb = _tile(Bs, 8, 1)
    while (nb * T) % 8 or Np % (nb * T):
        nb -= 1
    rows = nb * T
    cos_s, sin_s = _rope_tables_full(past + jnp.arange(T), RET_THETA, DK)
    cos_s, sin_s = jnp.tile(cos_s, (nb, 1)), jnp.tile(sin_s, (nb, 1))
    di, dqs, dks, dchs = decay(T)
    same = (jnp.arange(rows)[:, None] // T) == (jnp.arange(rows)[None, :] // T)
    di = jnp.where(same[None], jnp.tile(di, (1, nb, nb)), 0.0)
    dqs, dks = jnp.tile(dqs, (1, nb, 1)), jnp.tile(dks, (1, nb, 1))
    r0 = Np // rows
    o_s, s_s = pl.pallas_call(
        functools.partial(_ret_sample_kernel, scale=scale, nb=nb, t=T),
        grid=(Bs // nb, H),
        in_specs=[
            pl.BlockSpec((rows, DK), lambda b, h: (r0 + b, h)),
            pl.BlockSpec((rows, DK), lambda b, h: (r0 + b, H + h)),
            pl.BlockSpec((rows, DV), lambda b, h: (r0 + b, vblk0 + h)),
            pl.BlockSpec((rows, DV), lambda b, h: (r0 + b, vblk0 + H + h)),
            pl.BlockSpec((rows, half), lambda b, h: (0, 0)),
            pl.BlockSpec((rows, half), lambda b, h: (0, 0)),
            pl.BlockSpec((1, rows, rows), lambda b, h: (h, 0, 0)),
            pl.BlockSpec((1, rows, 1), lambda b, h: (h, 0, 0)),
            pl.BlockSpec((1, rows, 1), lambda b, h: (h, 0, 0)),
            pl.BlockSpec((1, 1, DV), lambda b, h: (h, 0, 0)),
            pl.BlockSpec((1, 1, DV), lambda b, h: (h, 0, 0)),
            pl.BlockSpec((1, nb, 1, DK, DV), lambda b, h: (layer, b, h, 0, 0)),
        ],
        out_specs=[
            pl.BlockSpec((rows, DV), lambda b, h: (b, h)),
            pl.BlockSpec((nb, 1, DK, DV), lambda b, h: (b, h, 0, 0)),
        ],
        out_shape=[jax.ShapeDtypeStruct((Ns, H * DV), BF16), jax.ShapeDtypeStruct((Bs, H, DK, DV), F32)],
        compiler_params=_cp("parallel", "parallel"),
        name="retention_sample",
    )(proj, proj, proj, proj, cos_s, sin_s, di, dqs, dks, dchs, gn3, state_s)
    return jnp.concatenate([o_p, o_s], 0), s_p, s_s


def _cm_gate_kernel(u_ref, v_ref, lg_ref, lb_ref, ws_ref, bs_ref, y_ref, vn_ref, *, groups):
    v = v_ref[...]
    mu = jnp.mean(v, -1, keepdims=True)
    d = v - mu
    var = jnp.mean(d * d, -1, keepdims=True)
    vn = d * lax.rsqrt(var + EPS) * lg_ref[...] + lb_ref[...]
    vn_ref[...] = vn
    cw = v.shape[1] // groups
    for g in range(groups):
        sl = slice(g * cw, (g + 1) * cw)
        s = jnp.dot(ws_ref[0, g].astype(BF16), vn[:, sl].astype(BF16), preferred_element_type=F32) + bs_ref[0, g]
        y_ref[:, sl] = (u_ref[:, sl] * s).astype(y_ref.dtype)


def _cm_gate(uv, ln_g, ln_b, w_s, b_s, dims):
    B, L, Bs, T, past = dims
    Np, Ns = B * L, Bs * T
    W = uv.shape[1] // 2
    G, C = w_s.shape[0], w_s.shape[1]
    assert L % C == 0 and Ns % C == 0 and C % T == 0 and T <= C
    npc = Np // C
    r = jnp.arange(C)
    tril = r[:, None] >= r[None, :]
    ws_p = jnp.where(tril[None], w_s, 0.0)
    tt = r % T
    same = (r[:, None] // T) == (r[None, :] // T)
    ws_s = jnp.where((same & (tt[:, None] >= tt[None, :]))[None], w_s[:, tt[:, None], tt[None, :]], 0.0)
    ws2 = jnp.stack([ws_p, ws_s])
    bs2 = jnp.stack([b_s, b_s[:, tt]])[..., None]
    kind = lambda i: jnp.where(i < npc, 0, 1)
    y, vn = pl.pallas_call(
        functools.partial(_cm_gate_kernel, groups=G),
        grid=((Np + Ns) // C,),
        in_specs=[
            pl.BlockSpec((C, W), lambda i: (i, 0)),
            pl.BlockSpec((C, W), lambda i: (i, 1)),
            pl.BlockSpec((1, W), lambda i: (0, 0)),
            pl.BlockSpec((1, W), lambda i: (0, 0)),
            pl.BlockSpec((1, G, C, C), lambda i: (kind(i), 0, 0, 0)),
            pl.BlockSpec((1, G, C, 1), lambda i: (kind(i), 0, 0, 0)),
        ],
        out_specs=[
            pl.BlockSpec((C, W), lambda i: (i, 0)),
            pl.BlockSpec((C, W), lambda i: (jnp.maximum(i - npc, 0), 0)),
        ],
        out_shape=[jax.ShapeDtypeStruct((Np + Ns, W), BF16), jax.ShapeDtypeStruct((Ns, W), F32)],
        compiler_params=_cp("arbitrary"),
        name="cm_gate",
    )(uv, uv, ln_g.reshape(1, W), ln_b.reshape(1, W), ws2, bs2)
    return y, vn


def _causal_widths(L, tq):
    if L % (4 * tq) == 0:
        return tuple(L * k // 4 for k in range(1, 5))
    return (L,)


def _for_causal_width(fn, need, widths):
    lo = 0
    for W in widths:
        @pl.when((need > lo) & (need <= W))
        def _(W=W):
            fn(W)
        lo = W


def _topk_mask(sc, valid, k, idx_bits):
    int_min = jnp.int32(-2 ** 31)
    kf = jnp.float32(k)
    bits = pltpu.bitcast(sc + 0.0, jnp.int32)
    key = jnp.where(bits < 0, bits ^ jnp.int32(0x7FFFFFFF), bits)
    key = jnp.where(valid, key, int_min)

    def count(m):
        return jnp.sum(m.astype(F32), axis=1, keepdims=True)

    t0 = jnp.where(count(key >= 0) >= kf, jnp.int32(0), int_min)

    def value_bit(i, t):
        cand = t | (jnp.int32(1) << (30 - i))
        return jnp.where(count(key >= cand) >= kf, cand, t)

    thr = lax.fori_loop(0, 31, value_bit, t0)
    gt = key > thr
    eq = key == thr
    need = kf - count(gt)
    idx = lax.broadcasted_iota(jnp.int32, sc.shape, 1)

    def index_bit(i, x):
        cand = x | (jnp.int32(1) << (idx_bits - 1 - i))
        return jnp.where(count(eq & (idx < cand)) < need, cand, x)

    last = lax.fori_loop(0, idx_bits, index_bit, jnp.zeros_like(thr))
    return valid & (gt | (eq & (idx <= last)))


def _dsa_prep_kernel(x_ref, qn_ref, kn_ref, c_ref, s1_ref, s2_ref, ci_ref, si1_ref, si2_ref,
                     q_ref, k_ref, kb_ref, v_ref, vb_ref, qi_ref, ki_ref, kib_ref, wh_ref,
                     *, heads, kv_heads, idx_heads, half, half_i, wh_scale):
    hd = LANES
    c, s1, s2 = c_ref[...], s1_ref[...], s2_ref[...]
    ci, si1, si2 = ci_ref[...], si1_ref[...], si2_ref[...]

    def normed(x, gain):
        return x * lax.rsqrt(jnp.mean(x * x, -1, keepdims=True) + EPS) * gain

    off = 0
    for h in range(heads):
        y = _rope_packed(normed(x_ref[:, off:off + hd], qn_ref[...]), c, s1, s2, half)
        q_ref[:, h * hd:(h + 1) * hd] = y.astype(q_ref.dtype)
        off += hd
    for g in range(kv_heads):
        y = _rope_packed(normed(x_ref[:, off:off + hd], kn_ref[...]), c, s1, s2, half)
        k_ref[:, g * hd:(g + 1) * hd] = y
        kb_ref[:, g * hd:(g + 1) * hd] = y.astype(BF16)
        off += hd
    v = x_ref[:, off:off + kv_heads * hd]
    v_ref[...] = v
    vb_ref[...] = v.astype(BF16)
    off += kv_heads * hd
    for h in range(idx_heads):
        y = _rope_packed(x_ref[:, off:off + hd], ci, si1, si2, half_i)
        qi_ref[:, h * hd:(h + 1) * hd] = y.astype(qi_ref.dtype)
        off += hd
    y = _rope_packed(x_ref[:, off:off + hd], ci, si1, si2, half_i)
    ki_ref[...] = y
    kib_ref[...] = y.astype(BF16)
    off += hd
    wh_ref[...] = x_ref[:, off:off + idx_heads] * wh_scale


def _dsa_prompt_kernel(q_ref, qi_ref, wh_ref, k_ref, v_ref, ki_ref, o_ref, bias_ref,
                       *, kv_heads, rep, idx_heads, k_top, idx_scale, scale, widths):
    hd = LANES
    tq = bias_ref.shape[0]
    dn = (((1,), (1,)), ((), ()))

    def attend(W):
        kib = ki_ref[0:W, :]
        wh = wh_ref[...] * idx_scale
        sc = jnp.zeros((tq, W), F32)
        for h in range(idx_heads):
            d = lax.dot_general(qi_ref[:, h * hd:(h + 1) * hd], kib, dn, preferred_element_type=F32)
            sc = sc + wh[:, h:h + 1] * jnp.maximum(d, 0.0)
        qpos = pl.program_id(1) * tq + lax.broadcasted_iota(jnp.int32, (tq, W), 0)
        spos = lax.broadcasted_iota(jnp.int32, (tq, W), 1)
        mask = _topk_mask(sc, spos <= qpos, k_top, max(1, (W - 1).bit_length()))
        bias_ref[:, 0:W] = jnp.where(mask, 0.0, -jnp.inf)
        for g in range(kv_heads):
            kg = k_ref[0:W, g * hd:(g + 1) * hd]
            vg = v_ref[0:W, g * hd:(g + 1) * hd]
            for r in range(rep):
                hq = g * rep + r
                s = lax.dot_general(q_ref[:, hq * hd:(hq + 1) * hd], kg, dn,
                                    preferred_element_type=F32) * scale + bias_ref[:, 0:W]
                e = jnp.exp(s - jnp.max(s, -1, keepdims=True))
                p = (e * (1.0 / jnp.sum(e, -1, keepdims=True))).astype(BF16)
                o_ref[:, hq * hd:(hq + 1) * hd] = jnp.dot(p, vg, preferred_element_type=F32).astype(o_ref.dtype)

    _for_causal_width(attend, (pl.program_id(1) + 1) * tq, widths)


def _paged_kernel(body, layer, planes, n_dense, n_out, pc, n_pages, page_rows):
    nchunks = n_pages // pc
    n_paged = len(planes)

    def kernel(pt_ref, *refs):
        paged = refs[:n_paged]
        dense = refs[n_paged:n_paged + n_dense]
        outs = refs[n_paged + n_dense:n_paged + n_dense + n_out]
        rest = refs[n_paged + n_dense + n_out:]
        bufs, sem, user = rest[:n_paged], rest[n_paged], rest[n_paged + 1:]
        c = pl.program_id(1)
        step = pl.program_id(0) * nchunks + c
        total = pl.num_programs(0) * nchunks
        slot = step % 2

        def copies(st, sl):
            base = (st // nchunks) * n_pages + (st % nchunks) * pc
            out = []
            for a in range(n_paged):
                for p in range(pc):
                    page = pt_ref[base + p]
                    rows = pl.ds(p * page_rows, page_rows)
                    if planes[a] is None:
                        out.append(pltpu.make_async_copy(paged[a].at[layer, page], bufs[a].at[sl, rows], sem.at[sl, a]))
                    elif planes[a] == "T":
                        out.append(pltpu.make_async_copy(paged[a].at[layer, page], bufs[a].at[sl, :, rows], sem.at[sl, a]))
                    else:
                        for g in range(planes[a]):
                            out.append(pltpu.make_async_copy(paged[a].at[layer, page, :, g],
                                                             bufs[a].at[sl, g, rows], sem.at[sl, a]))
            return out

        @pl.when(step == 0)
        def _():
            for cp in copies(step, slot):
                cp.start()

        @pl.when(step + 1 < total)
        def _():
            for cp in copies(step + 1, 1 - slot):
                cp.start()

        for cp in copies(step, slot):
            cp.wait()
        body(c, nchunks, [bufs[a].at[slot] for a in range(n_paged)], dense, outs, user)

    return kernel


def _paged_call(body, page_table, layer, paged, dense, dense_specs, out_shapes, out_specs, user_scratch, pc, name,
                transposed=()):
    Bs, n_pages = page_table.shape
    page_rows = paged[0].shape[2]
    nchunks = n_pages // pc
    planes = ["T" if i in transposed else (a.shape[3] if a.ndim == 5 else None) for i, a in enumerate(paged)]

    def window(a, g):
        if g is None:
            return (2, pc * page_rows, a.shape[3])
        if g == "T":
            return (2, a.shape[2], pc * page_rows)
        return (2, g, pc * page_rows, a.shape[4])

    kernel = _paged_kernel(body, layer, planes, len(dense), 1, pc, n_pages, page_rows)
    scratch = [pltpu.VMEM(window(a, g), a.dtype) for a, g in zip(paged, planes)]
    scratch.append(pltpu.SemaphoreType.DMA((2, len(paged))))
    scratch.extend(user_scratch)
    grid_spec = pltpu.PrefetchScalarGridSpec(
        num_scalar_prefetch=1,
        grid=(Bs, nchunks),
        in_specs=[pl.BlockSpec(memory_space=pl.ANY)] * len(paged) + list(dense_specs),
        out_specs=out_specs,
        scratch_shapes=scratch,
    )
    return pl.pallas_call(
        kernel, grid_spec=grid_spec, out_shape=out_shapes,
        compiler_params=_cp("arbitrary", "arbitrary"), name=name,
    )(page_table.reshape(-1), *paged, *dense)


def _dsa_sample_select_body(c, nchunks, bufs, dense, outs, user, *, t, idx_heads, k_top, idx_bits, idx_scale):
    (ki_buf,) = bufs
    qi_ref, wh_ref, kin_ref = dense
    (bias_ref,) = outs
    (sc_ref,) = user
    w = ki_buf.shape[0]
    past = nchunks * w

    def scores(keys_bf16):
        d = lax.dot_general(qi_ref[0], keys_bf16, (((1,), (1,)), ((), ())), preferred_element_type=F32)
        wd = jnp.maximum(d, 0.0) * (wh_ref[0] * idx_scale)
        return jnp.sum(wd.reshape(t, idx_heads, wd.shape[1]), axis=1)

    group, slab, S = sc_ref.shape
    gi = pl.program_id(0) % group

    @pl.when((c == 0) & (gi == 0))
    def _():
        sc_ref[...] = jnp.zeros_like(sc_ref)

    sc_ref[gi, 0:t, pl.ds(pl.multiple_of(c * w, LANES), w)] = scores(ki_buf[...].astype(BF16))

    @pl.when(c == nchunks - 1)
    def _():
        sc_ref[gi, 0:t, past:past + LANES] = scores(kin_ref[0])

    @pl.when((c == nchunks - 1) & (gi == group - 1))
    def _():
        rows = group * slab
        row = lax.broadcasted_iota(jnp.int32, (rows, S), 0) % slab
        col = lax.broadcasted_iota(jnp.int32, (rows, S), 1)
        valid = (col < past) | ((col - past <= row) & (col - past < t))
        mask = _topk_mask(sc_ref[...].reshape(rows, S), valid, k_top, idx_bits)
        bias_ref[...] = jnp.where(mask, 0.0, NEG).reshape(group, slab, S)[:, 0:t]


def _expand_rows(x, reps):
    t, w = x.shape
    row = lax.broadcasted_iota(jnp.int32, (t * reps, w), 0) // reps
    out = jnp.broadcast_to(x[0:1], (t * reps, w))
    for i in range(1, t):
        out = jnp.where(row == i, jnp.broadcast_to(x[i:i + 1], (t * reps, w)), out)
    return out


def _softmax_update(s, v_bf16, m_ref, l_ref, acc_ref, idx):
    m_old = m_ref[idx]
    m_new = jnp.maximum(m_old, jnp.max(s, -1, keepdims=True))
    a = jnp.exp(m_old - m_new)
    p = jnp.exp(s - m_new)
    l_ref[idx] = l_ref[idx] * a + jnp.sum(p, -1, keepdims=True)
    acc_ref[idx] = acc_ref[idx] * a + jnp.dot(p.astype(BF16), v_bf16, preferred_element_type=F32)
    m_ref[idx] = m_new


def _dsa_sample_attend_body(c, nchunks, bufs, dense, outs, user, *, kv_heads, rep, scale):
    k_buf, v_buf = bufs
    q_ref, bias_ref, tail_ref, kn_ref, vn_ref = dense
    (o_ref,) = outs
    m_ref, l_ref, acc_ref = user
    hd = LANES

    @pl.when(c == 0)
    def _():
        m_ref[...] = jnp.full_like(m_ref, NEG)
        l_ref[...] = jnp.zeros_like(l_ref)
        acc_ref[...] = jnp.zeros_like(acc_ref)

    def attend(keys, vals, bias):
        bias = _expand_rows(bias, rep)
        for g in range(kv_heads):
            s = lax.dot_general(q_ref[0, g], keys(g), (((1,), (1,)), ((), ())),
                                preferred_element_type=F32) * scale + bias
            _softmax_update(s, vals(g), m_ref, l_ref, acc_ref, g)

    attend(lambda g: k_buf[g].astype(BF16), lambda g: v_buf[g].astype(BF16), bias_ref[0])

    @pl.when(c == nchunks - 1)
    def _():
        attend(lambda g: kn_ref[0, :, g * hd:(g + 1) * hd], lambda g: vn_ref[0, :, g * hd:(g + 1) * hd], tail_ref[0])
        o_ref[0] = (acc_ref[...] / l_ref[...]).astype(o_ref.dtype)


def _dsa(proj, caches, layer, page_table, q_norm, k_norm, dims, dsa_heads):
    cache_k, cache_v, cache_ki = caches
    B, L, Bs, T, past = dims
    Np, Ns = B * L, Bs * T
    N = Np + Ns
    _, n_pool, page, KVH, HD = cache_k.shape
    IDX_DIM = cache_ki.shape[-1]
    H = dsa_heads
    REP = H // KVH
    IDXH = (proj.shape[1] - H * HD - 2 * KVH * HD - IDX_DIM) // (IDX_DIM + 1)
    assert HD == LANES and IDX_DIM == LANES
    rope_dims, idx_rope = HD // 4, IDX_DIM // 4
    idx_scale = IDX_DIM ** -0.5
    scale = HD ** -0.5

    pos = jnp.concatenate([jnp.arange(L), jnp.tile(past + jnp.arange(T), Bs)])
    tabs = _rope_tables_packed(pos, DSA_THETA, rope_dims, HD) + _rope_tables_packed(pos, DSA_THETA, idx_rope, IDX_DIM)
    tm = _tile(math.gcd(L, Ns), 256)
    lpt, npt = L // tm, Np // tm
    tab_spec = pl.BlockSpec((tm, LANES), lambda i: (jnp.where(i < npt, i % lpt, lpt + i - npt), 0))
    rowspec = lambda w: pl.BlockSpec((tm, w), lambda i: (i, 0))
    shapes = [(H * HD, BF16), (KVH * HD, F32), (KVH * HD, BF16), (KVH * HD, F32), (KVH * HD, BF16),
              (IDXH * IDX_DIM, BF16), (IDX_DIM, F32), (IDX_DIM, BF16), (IDXH, F32)]
    q, k, kb, v, vb, qi, ki, kib, wh = pl.pallas_call(
        functools.partial(_dsa_prep_kernel, heads=H, kv_heads=KVH, idx_heads=IDXH, half=rope_dims // 2,
                          half_i=idx_rope // 2, wh_scale=IDXH ** -0.5),
        grid=(N // tm,),
        in_specs=[rowspec(proj.shape[1]), pl.BlockSpec((1, HD), lambda i: (0, 0)),
                  pl.BlockSpec((1, HD), lambda i: (0, 0))] + [tab_spec] * 6,
        out_specs=[rowspec(w) for w, _ in shapes],
        out_shape=[jax.ShapeDtypeStruct((N, w), dt) for w, dt in shapes],
        compiler_params=_cp("parallel"),
        name="dsa_prep",
    )(proj, q_norm.reshape(1, HD), k_norm.reshape(1, HD), *tabs)

    tq = Q_BLOCK
    nq = L // tq
    k_top = min(IDX_TOPK, L // 4)
    o_p = pl.pallas_call(
        functools.partial(_dsa_prompt_kernel, kv_heads=KVH, rep=REP, idx_heads=IDXH, k_top=k_top,
                          idx_scale=idx_scale, scale=scale, widths=_causal_widths(L, tq)),
        grid=(B, nq),
        in_specs=[
            pl.BlockSpec((tq, H * HD), lambda b, i: (b * nq + i, 0)),
            pl.BlockSpec((tq, IDXH * IDX_DIM), lambda b, i: (b * nq + i, 0)),
            pl.BlockSpec((tq, IDXH), lambda b, i: (b * nq + i, 0)),
            pl.BlockSpec((L, KVH * HD), lambda b, i: (b, 0)),
            pl.BlockSpec((L, KVH * HD), lambda b, i: (b, 0)),
            pl.BlockSpec((L, IDX_DIM), lambda b, i: (b, 0)),
        ],
        out_specs=pl.BlockSpec((tq, H * HD), lambda b, i: (b * nq + i, 0)),
        out_shape=jax.ShapeDtypeStruct((Np, H * HD), BF16),
        scratch_shapes=[pltpu.VMEM((tq, L), F32)],
        compiler_params=_cp("parallel", "arbitrary"),
        name="dsa_prompt",
    )(q, qi, wh, kb, vb, kib)

    n_pages = page_table.shape[1]
    pc = _tile(n_pages, 16, 1)
    w = pc * page
    S = past + LANES
    k_top_s = min(IDX_TOPK, (past + T) // 4)
    rows = 8
    group = _tile(Bs, 8, 1)
    assert T <= rows
    qi_s = qi[Np:].reshape(Bs, T * IDXH, IDX_DIM)
    wh_s = wh[Np:].reshape(Bs, T * IDXH, 1)
    pad_new = lambda x: jnp.pad(x[Np:].reshape(Bs, T, x.shape[1]), ((0, 0), (0, LANES - T), (0, 0)))
    bias = _paged_call(
        functools.partial(_dsa_sample_select_body, t=T, idx_heads=IDXH, k_top=k_top_s,
                          idx_bits=(S - 1).bit_length(), idx_scale=idx_scale),
        page_table, layer, [cache_ki],
        [qi_s, wh_s, pad_new(kib)],
        [pl.BlockSpec((1, T * IDXH, IDX_DIM), lambda b, c, pt: (b, 0, 0)),
         pl.BlockSpec((1, T * IDXH, 1), lambda b, c, pt: (b, 0, 0)),
         pl.BlockSpec((1, LANES, IDX_DIM), lambda b, c, pt: (b, 0, 0))],
        jax.ShapeDtypeStruct((Bs, T, S), F32),
        pl.BlockSpec((group, T, S), lambda b, c, pt: (b // group, 0, 0)),
        [pltpu.VMEM((group, rows, S), F32)],
        _tile(n_pages, 32, 1), "dsa_sample_select")

    q_s = q[Np:].reshape(Bs, T, KVH, REP, HD).transpose(0, 2, 1, 3, 4).reshape(Bs, KVH, T * REP, HD)
    o_s = _paged_call(
        functools.partial(_dsa_sample_attend_body, kv_heads=KVH, rep=REP, scale=scale),
        page_table, layer, [cache_k, cache_v],
        [q_s, bias, bias, pad_new(kb), pad_new(vb)],
        [pl.BlockSpec((1, KVH, T * REP, HD), lambda b, c, pt: (b, 0, 0, 0)),
         pl.BlockSpec((1, T, w), lambda b, c, pt: (b, 0, c)),
         pl.BlockSpec((1, T, LANES), lambda b, c, pt: (b, 0, past // LANES)),
         pl.BlockSpec((1, LANES, KVH * HD), lambda b, c, pt: (b, 0, 0)),
         pl.BlockSpec((1, LANES, KVH * HD), lambda b, c, pt: (b, 0, 0))],
        jax.ShapeDtypeStruct((Bs, KVH, T * REP, HD), BF16),
        pl.BlockSpec((1, KVH, T * REP, HD), lambda b, c, pt: (b, 0, 0, 0)),
        [pltpu.VMEM((KVH, T * REP, 1), F32), pltpu.VMEM((KVH, T * REP, 1), F32),
         pltpu.VMEM((KVH, T * REP, HD), F32)],
        pc, "dsa_sample_attend")
    o_s = o_s.reshape(Bs, KVH, T, REP, HD).transpose(0, 2, 1, 3, 4).reshape(Ns, H * HD)
    return jnp.concatenate([o_p, o_s], 0), k, v, ki


def _norm_rope_pairs(x, gain, c, s1, s2, group, half):
    lane = lax.broadcasted_iota(jnp.int32, x.shape, 1)
    x2 = x * x
    inv = jnp.zeros_like(x)
    for j in range(LANES // group):
        sel = (lane >= j * group) & (lane < (j + 1) * group)
        ms = jnp.sum(jnp.where(sel, x2, 0.0), -1, keepdims=True) * (1.0 / group)
        inv = jnp.where(sel, lax.rsqrt(ms + EPS), inv)
    return _rope_packed(x * inv * gain, c, s1, s2, half)


def _mla_prep_kernel(x_ref, qa_ref, kvn_ref, kpn_ref, c_ref, s1_ref, s2_ref,
                     cq_ref, ckv_ref, ckvb_ref, kpe_ref, kpeb_ref, *, q_lora, kv_lora, rope):
    def normed(x, gain):
        return x * lax.rsqrt(jnp.mean(x * x, -1, keepdims=True) + EPS) * gain

    cq_ref[...] = normed(x_ref[:, 0:q_lora], qa_ref[...]).astype(cq_ref.dtype)
    ckv = normed(x_ref[:, q_lora:q_lora + kv_lora], kvn_ref[...])
    ckv_ref[...] = ckv
    ckvb_ref[...] = ckv.astype(BF16)
    y = _norm_rope_pairs(x_ref[:, q_lora + kv_lora:q_lora + kv_lora + LANES], kpn_ref[...],
                         c_ref[...], s1_ref[...], s2_ref[...], rope, rope // 2)
    kpe_ref[...] = y[:, 0:rope]
    kpeb_ref[...] = y[:, 0:rope].astype(BF16)


def _mla_q_kernel(x_ref, gn_ref, gp_ref, c_ref, s1_ref, s2_ref, qn_ref, qp_ref, *, heads, nope, rope):
    def normed(x, gain):
        return x * lax.rsqrt(jnp.mean(x * x, -1, keepdims=True) + EPS) * gain

    for h in range(heads):
        qn_ref[:, h * nope:(h + 1) * nope] = normed(x_ref[:, h * nope:(h + 1) * nope], gn_ref[...]).astype(qn_ref.dtype)
    base = heads * nope
    c, s1, s2 = c_ref[...], s1_ref[...], s2_ref[...]
    for j in range(heads * rope // LANES):
        y = _norm_rope_pairs(x_ref[:, base + j * LANES:base + (j + 1) * LANES], gp_ref[...], c, s1, s2, rope, rope // 2)
        qp_ref[:, j * LANES:(j + 1) * LANES] = y.astype(qp_ref.dtype)


def _mla_prompt_kernel(qn_ref, qp_ref, kn_ref, kp_ref, v_ref, o_ref, *, heads, nope, rope, vdim, scale, widths):
    tq = qn_ref.shape[0]
    dn = (((1,), (1,)), ((), ()))
    per_tile = LANES // rope

    def attend(W):
        qpos = pl.program_id(1) * tq + lax.broadcasted_iota(jnp.int32, (tq, W), 0)
        spos = lax.broadcasted_iota(jnp.int32, (tq, W), 1)
        causal = spos <= qpos
        kp = kp_ref[0:W, :]
        lane = lax.broadcasted_iota(jnp.int32, (tq, LANES), 1)
        for h in range(heads):
            j, sub = h // per_tile, h % per_tile
            qp = qp_ref[:, j * LANES:(j + 1) * LANES]
            qp = jnp.where((lane >= sub * rope) & (lane < (sub + 1) * rope), qp, jnp.zeros_like(qp))
            s = (lax.dot_general(qn_ref[:, h * nope:(h + 1) * nope], kn_ref[0:W, h * nope:(h + 1) * nope], dn,
                                 preferred_element_type=F32)
                 + lax.dot_general(qp, kp, dn, preferred_element_type=F32)) * scale
            s = jnp.where(causal, s, -jnp.inf)
            e = jnp.exp(s - jnp.max(s, -1, keepdims=True))
            p = (e * (1.0 / jnp.sum(e, -1, keepdims=True))).astype(BF16)
            o_ref[:, h * vdim:(h + 1) * vdim] = jnp.dot(p, v_ref[0:W, h * vdim:(h + 1) * vdim],
                                                        preferred_element_type=F32).astype(o_ref.dtype)

    _for_causal_width(attend, (pl.program_id(1) + 1) * tq, widths)


def _mla_sample_body(c, nchunks, bufs, dense, outs, user, *, t, heads, scale):
    c_buf, r_buf = bufs
    ql_ref, qp_ref, cn_ref, rn_ref = dense
    (o_ref,) = outs
    m_ref, l_ref, acc_ref = user
    dn = (((1,), (1,)), ((), ()))

    @pl.when(c == 0)
    def _():
        m_ref[...] = jnp.full_like(m_ref, NEG)
        l_ref[...] = jnp.zeros_like(l_ref)
        acc_ref[...] = jnp.zeros_like(acc_ref)

    lat = c_buf[...].astype(BF16)
    s = (lax.dot_general(ql_ref[0], lat, dn, preferred_element_type=F32)
         + jnp.dot(qp_ref[0], r_buf[...].astype(BF16), preferred_element_type=F32)) * scale
    _softmax_update(s, lat, m_ref, l_ref, acc_ref, 0)

    @pl.when(c == nchunks - 1)
    def _():
        lat_n = cn_ref[0]
        s = (lax.dot_general(ql_ref[0], lat_n, dn, preferred_element_type=F32)
             + lax.dot_general(qp_ref[0], rn_ref[0], dn, preferred_element_type=F32)) * scale
        trow = lax.broadcasted_iota(jnp.int32, s.shape, 0) // heads
        col = lax.broadcasted_iota(jnp.int32, s.shape, 1)
        s = jnp.where((col <= trow) & (col < t), s, NEG)
        _softmax_update(s, lat_n, m_ref, l_ref, acc_ref, 0)
        o_ref[0] = (acc_ref[0] / l_ref[0]).astype(o_ref.dtype)


def _mla(xn, caches, layer, page_table, w_in, qa_norm, kv_norm, w_uq, qn_nope, qn_pe, kpe_norm, w_uk, w_uv, dims, tm):
    cache_ckv, cache_kpe = caches
    B, L, Bs, T, past = dims
    Np, Ns = B * L, Bs * T
    N = Np + Ns
    KV = cache_ckv.shape[-1]
    R = cache_kpe.shape[-1]
    QL = w_uq.shape[0]
    H, NOPE = w_uk.shape[1], w_uk.shape[2]
    VD = w_uv.shape[2]
    assert NOPE == LANES and LANES % R == 0 and (H * R) % LANES == 0 and (QL + KV) % LANES == 0
    scale = (NOPE + R) ** -0.5

    w_in_p = jnp.pad(w_in, ((0, 0), (0, LANES - R)))
    proj = _matmul(xn, w_in_p, tm=tm, name="mla_in")
    pos = jnp.concatenate([jnp.arange(L), jnp.tile(past + jnp.arange(T), Bs)])
    tabs = _rope_tables_packed(pos, MLA_THETA, R, R)
    tp = _tile(math.gcd(L, Ns), 256)
    lpt, npt = L // tp, Np // tp
    tab_spec = pl.BlockSpec((tp, LANES), lambda i: (jnp.where(i < npt, i % lpt, lpt + i - npt), 0))
    rowspec = lambda w: pl.BlockSpec((tp, w), lambda i: (i, 0))
    vec = lambda w: pl.BlockSpec((1, w), lambda i: (0, 0))
    tile_gain = lambda g: jnp.tile(g, LANES // R).reshape(1, LANES)
    shapes = [(QL, BF16), (KV, F32), (KV, BF16), (R, F32), (R, BF16)]
    cq, ckv, ckvb, kpe, kpeb = pl.pallas_call(
        functools.partial(_mla_prep_kernel, q_lora=QL, kv_lora=KV, rope=R),
        grid=(N // tp,),
        in_specs=[rowspec(proj.shape[1]), vec(QL), vec(KV), vec(LANES)] + [tab_spec] * 3,
        out_specs=[rowspec(w) for w, _ in shapes],
        out_shape=[jax.ShapeDtypeStruct((N, w), dt) for w, dt in shapes],
        compiler_params=_cp("parallel"),
        name="mla_prep",
    )(proj, qa_norm.reshape(1, QL), kv_norm.reshape(1, KV), tile_gain(kpe_norm), *tabs)

    w3 = w_uq.reshape(QL, H, NOPE + R)
    w_uq_p = jnp.concatenate([w3[:, :, :NOPE].reshape(QL, H * NOPE), w3[:, :, NOPE:].reshape(QL, H * R)], 1)
    qraw = _matmul(cq, w_uq_p, tm=tm, name="mla_uq")
    q_nope, q_pe = pl.pallas_call(
        functools.partial(_mla_q_kernel, heads=H, nope=NOPE, rope=R),
        grid=(N // tp,),
        in_specs=[rowspec(qraw.shape[1]), vec(NOPE), vec(LANES)] + [tab_spec] * 3,
        out_specs=[rowspec(H * NOPE), rowspec(H * R)],
        out_shape=[jax.ShapeDtypeStruct((N, H * NOPE), BF16), jax.ShapeDtypeStruct((N, H * R), BF16)],
        compiler_params=_cp("parallel"),
        name="mla_q",
    )(qraw, qn_nope.reshape(1, NOPE), tile_gain(qn_pe), *tabs)

    tk = _tile(Np, 512)
    k_nope = _matmul(ckvb[:Np], w_uk.reshape(KV, H * NOPE), tm=tk, out_dtype=BF16, name="mla_uk")
    v = _matmul(ckvb[:Np], w_uv.reshape(KV, H * VD), tm=tk, out_dtype=BF16, name="mla_uv")
    tq = Q_BLOCK
    nq = L // tq
    o_p = pl.pallas_call(
        functools.partial(_mla_prompt_kernel, heads=H, nope=NOPE, rope=R, vdim=VD, scale=scale,
                          widths=_causal_widths(L, tq)),
        grid=(B, nq),
        in_specs=[
            pl.BlockSpec((tq, H * NOPE), lambda b, i: (b * nq + i, 0)),
            pl.BlockSpec((tq, H * R), lambda b, i: (b * nq + i, 0)),
            pl.BlockSpec((L, H * NOPE), lambda b, i: (b, 0)),
            pl.BlockSpec((L, LANES), lambda b, i: (b, 0)),
            pl.BlockSpec((L, H * VD), lambda b, i: (b, 0)),
        ],
        out_specs=pl.BlockSpec((tq, H * VD), lambda b, i: (b * nq + i, 0)),
        out_shape=jax.ShapeDtypeStruct((Np, H * VD), BF16),
        compiler_params=_cp("parallel", "arbitrary"),
        name="mla_prompt",
    )(q_nope, q_pe, k_nope, jnp.tile(kpeb[:Np], (1, LANES // R)), v)

    ts = _tile(Ns, 512)
    q_lat = _head_mm(q_nope[Np:], w_uk.reshape(KV, H * NOPE), H, trans_w=True, tm=ts, out_dtype=BF16,
                     name="mla_absorb_q")
    n_pages = page_table.shape[1]
    pc = _tile(n_pages, 32, 1)
    pad_new = lambda x: jnp.pad(x[Np:].reshape(Bs, T, x.shape[1]), ((0, 0), (0, LANES - T), (0, 0)))
    o_lat = _paged_call(
        functools.partial(_mla_sample_body, t=T, heads=H, scale=scale),
        page_table, layer, [cache_ckv, jnp.swapaxes(cache_kpe, 2, 3)],
        [q_lat.reshape(Bs, T * H, KV), q_pe[Np:].reshape(Bs, T * H, R), pad_new(ckvb), pad_new(kpeb)],
        [pl.BlockSpec((1, T * H, KV), lambda b, c, pt: (b, 0, 0)),
         pl.BlockSpec((1, T * H, R), lambda b, c, pt: (b, 0, 0)),
         pl.BlockSpec((1, LANES, KV), lambda b, c, pt: (b, 0, 0)),
         pl.BlockSpec((1, LANES, R), lambda b, c, pt: (b, 0, 0))],
        jax.ShapeDtypeStruct((Bs, T * H, KV), BF16),
        pl.BlockSpec((1, T * H, KV), lambda b, c, pt: (b, 0, 0)),
        [pltpu.VMEM((1, T * H, 1), F32), pltpu.VMEM((1, T * H, 1), F32), pltpu.VMEM((1, T * H, KV), F32)],
        pc, "mla_sample", transposed=(1,))
    o_s = _head_mm(o_lat.reshape(Ns, H * KV), w_uv.reshape(KV, H * VD), H, trans_w=False, tm=ts,
                   out_dtype=BF16, name="mla_absorb_o")
    return jnp.concatenate([o_p, o_s], 0), ckv, kpe


def _moe_count_kernel(e_ref, cnt_ref):
    @pl.when(pl.program_id(0) == 0)
    def _():
        cnt_ref[...] = jnp.zeros_like(cnt_ref)

    e = e_ref[0]
    hit = lax.broadcasted_iota(jnp.int32, (cnt_ref.shape[0], e.shape[1]), 0) == e
    cnt_ref[...] += jnp.sum(hit.astype(F32), axis=1, keepdims=True)


def _moe_dest_kernel(e_ref, pstart_ref, dest_ref, run_ref):
    @pl.when(pl.program_id(0) == 0)
    def _():
        run_ref[...] = pstart_ref[...]

    e = e_ref[0]
    ch = e.shape[1]
    hit = lax.broadcasted_iota(jnp.int32, (run_ref.shape[0], ch), 0) == e
    tri = (lax.broadcasted_iota(jnp.int32, (ch, ch), 0) <= lax.broadcasted_iota(jnp.int32, (ch, ch), 1))
    prefix = jnp.dot(hit.astype(BF16), tri.astype(BF16), preferred_element_type=F32)
    pos = prefix - 1.0 + run_ref[...]
    dest_ref[0] = jnp.sum(jnp.where(hit, pos, 0.0), axis=0, keepdims=True).astype(jnp.int32)
    run_ref[...] += jnp.sum(hit.astype(F32), axis=1, keepdims=True)


def _moe_kernel(be_ref, first_ref, nxt_ref, par_ref, nblk_ref, tok_ref, x_hbm, g_ref, w1_hbm, w3_hbm, w2_hbm,
                o_ref, xbuf, wf1, wf3, wf2, w1b, w3b, w2b, xsem, wsem, *, layer, rows):
    i = pl.program_id(0)
    n = nblk_ref[0]
    slot = i % ROW_SLOTS

    def x_copy(blk, sl, r):
        return pltpu.make_async_copy(x_hbm.at[pl.ds(tok_ref[blk * rows + r], 1)], xbuf.at[sl, pl.ds(r, 1)], xsem.at[sl])

    def start_rows(blk, sl):
        def issue(r, carry):
            x_copy(blk, sl, r).start()
            return carry
        lax.fori_loop(0, rows, issue, 0, unroll=8)

    def w_copies(e, sl):
        return [pltpu.make_async_copy(w1_hbm.at[layer, e], wf1.at[sl], wsem.at[sl, 0]),
                pltpu.make_async_copy(w3_hbm.at[layer, e], wf3.at[sl], wsem.at[sl, 1]),
                pltpu.make_async_copy(w2_hbm.at[layer, e], wf2.at[sl], wsem.at[sl, 2])]

    @pl.when(i == 0)
    def _():
        start_rows(0, 0)
        for cp in w_copies(be_ref[0], 0):
            cp.start(priority=WEIGHT_DMA_PRIORITY)

        @pl.when(1 < n)
        def _():
            start_rows(1, 1)

    @pl.when(i < n)
    def _():
        @pl.when(i + 2 < n)
        def _():
            start_rows(i + 2, (i + 2) % ROW_SLOTS)

        @pl.when(first_ref[i] == 1)
        def _():
            p = par_ref[i]

            @pl.when(nxt_ref[i] >= 0)
            def _():
                for cp in w_copies(nxt_ref[i], 1 - p):
                    cp.start(priority=WEIGHT_DMA_PRIORITY)

            for cp in w_copies(be_ref[i], p):
                cp.wait()
            w1b[...] = wf1[p].astype(BF16)
            w3b[...] = wf3[p].astype(BF16)
            w2b[...] = wf2[p].astype(BF16)

        def wait_row(r, carry):
            x_copy(i, slot, r).wait()
            return carry
        lax.fori_loop(0, rows, wait_row, 0, unroll=8)
        x = xbuf[slot].astype(BF16)
        h = jax.nn.silu(jnp.dot(x, w1b[...], preferred_element_type=F32)) * jnp.dot(x, w3b[...], preferred_element_type=F32)
        y = jnp.dot(h.astype(BF16), w2b[...], preferred_element_type=F32)
        o_ref[...] = y * g_ref[...]

    @pl.when(i >= n)
    def _():
        o_ref[...] = jnp.zeros_like(o_ref)


def _moe(h, xn, w_group, b_group, w_expert, b_expert, w1, w3, w2, layer, tm):
    N, D = xn.shape
    G = w_group.shape[1]
    E = w_expert.shape[1]
    EPG = E // G
    FF = w1.shape[3]
    R = MOE_ROWS
    logits = _matmul(xn, jnp.concatenate([w_group, w_expert], 1), tm=tm, name="moe_router")
    tok = jnp.arange(N)
    lg = logits[:, :G] + b_group
    grp = jnp.argmax(lg, -1)
    p_grp = jax.nn.softmax(lg, -1)[tok, grp]
    le = (logits[:, G:] + b_expert).reshape(N, G, EPG)[tok, grp]
    top_v, top_i = lax.top_k(le, MOE_TOPK)
    gates = (p_grp[:, None] * jax.nn.softmax(top_v, -1)).reshape(-1)
    experts = (grp[:, None] * EPG + top_i).reshape(-1).astype(jnp.int32)
    A = N * MOE_TOPK
    ch = _tile(A, 512, LANES)
    e3 = experts.reshape(A // ch, 1, ch)
    chunk_spec = pl.BlockSpec((1, 1, ch), lambda c: (c, 0, 0))
    col_spec = pl.BlockSpec((E, 1), lambda c: (0, 0))
    counts = pl.pallas_call(
        _moe_count_kernel, grid=(A // ch,), in_specs=[chunk_spec], out_specs=col_spec,
        out_shape=jax.ShapeDtypeStruct((E, 1), F32), compiler_params=_cp("arbitrary"), name="moe_count",
    )(e3)[:, 0].astype(jnp.int32)
    pcounts = (counts + R - 1) // R * R
    pends = jnp.cumsum(pcounts)
    pstarts = pends - pcounts
    dest = pl.pallas_call(
        _moe_dest_kernel, grid=(A // ch,), in_specs=[chunk_spec, col_spec], out_specs=chunk_spec,
        out_shape=jax.ShapeDtypeStruct((A // ch, 1, ch), jnp.int32),
        scratch_shapes=[pltpu.VMEM((E, 1), F32)], compiler_params=_cp("arbitrary"), name="moe_dest",
    )(e3, pstarts.astype(F32).reshape(E, 1)).reshape(A)
    n_blocks = -(-A // R) + E
    P = n_blocks * R
    blk = jnp.arange(n_blocks)
    blk_e = jnp.minimum(jnp.sum((pends[None, :] <= (blk * R)[:, None]).astype(jnp.int32), axis=1), E - 1)
    n_used = pends[-1] // R
    first = (jnp.concatenate([jnp.ones((1,), bool), blk_e[1:] != blk_e[:-1]]) & (blk < n_used)).astype(jnp.int32)
    par = (jnp.cumsum(first) - 1) % 2
    first_pos = jnp.where(first == 1, blk, n_blocks)
    nfp = lax.cummin(jnp.concatenate([first_pos[1:], jnp.full((1,), n_blocks)]), reverse=True)
    nxt = jnp.where(nfp < n_blocks, blk_e[jnp.minimum(nfp, n_blocks - 1)], -1)
    a_pad = jnp.full((P,), -1, jnp.int32).at[dest].set(jnp.arange(A, dtype=jnp.int32))
    src = jnp.maximum(a_pad, 0)
    tok_pad = src // MOE_TOPK
    gate_pad = jnp.where(a_pad >= 0, gates[src], 0.0)
    i32 = lambda x: x.astype(jnp.int32)
    grid_spec = pltpu.PrefetchScalarGridSpec(
        num_scalar_prefetch=6,
        grid=(n_blocks,),
        in_specs=[
            pl.BlockSpec(memory_space=pl.ANY),
            pl.BlockSpec((R, 1), lambda i, *_: (i, 0)),
            pl.BlockSpec(memory_space=pl.ANY),
            pl.BlockSpec(memory_space=pl.ANY),
            pl.BlockSpec(memory_space=pl.ANY),
        ],
        out_specs=pl.BlockSpec((R, D), lambda i, *_: (i, 0)),
        scratch_shapes=[pltpu.VMEM((ROW_SLOTS, R, D), F32),
                        pltpu.VMEM((2, D, FF), F32), pltpu.VMEM((2, D, FF), F32), pltpu.VMEM((2, FF, D), F32),
                        pltpu.VMEM((D, FF), BF16), pltpu.VMEM((D, FF), BF16), pltpu.VMEM((FF, D), BF16),
                        pltpu.SemaphoreType.DMA((ROW_SLOTS,)), pltpu.SemaphoreType.DMA((2, 3))],
    )
    yb = pl.pallas_call(
        functools.partial(_moe_kernel, layer=layer, rows=R), grid_spec=grid_spec,
        out_shape=jax.ShapeDtypeStruct((P, D), F32),
        compiler_params=_cp("arbitrary"), name="moe_experts",
    )(i32(blk_e), i32(first), i32(nxt), i32(par), i32(n_used).reshape(1), i32(tok_pad),
      xn, gate_pad.reshape(P, 1), w1, w3, w2)
    d2 = dest.reshape(N, MOE_TOPK)
    return h + (yb[d2[:, 0]] + yb[d2[:, 1]])


def kernel(x_prompt, x_sample, state_ret, cache_k_c, cache_v_c, cache_kidx_c, cache_ckv_d, cache_kpe_d, page_table, norm_mix, norm_ffn, ret_w_in, ret_gn, ret_w_o, cm_w_in, cm_ln_g, cm_ln_b, cm_w_s, cm_b_s, cm_w_o, dsa_w_in, dsa_q_norm, dsa_k_norm, dsa_w_o, mla_w_in, mla_qa_norm, mla_kv_norm, mla_w_uq, mla_qn_nope, mla_qn_pe, mla_kpe_norm, mla_w_uk, mla_w_uv, mla_w_o, moe_w_group, moe_b_group, moe_w_expert, moe_b_expert, moe_w1, moe_w3, moe_w2):
    B, L, D = x_prompt.shape
    Bs, T, _ = x_sample.shape
    past = page_table.shape[1] * cache_k_c.shape[2]
    dims = (B, L, Bs, T, past)
    Np, Ns = B * L, Bs * T
    N = Np + Ns
    tm = _tile(N, 512, 16)
    depth = norm_mix.shape[0]
    h = jnp.concatenate([x_prompt.reshape(Np, D), x_sample.reshape(Ns, D)], 0)
    outs = {k: [] for k in ("ret_p", "ret_s", "cm_s", "kc", "vc", "ic", "ckv", "kpe")}
    for i in range(depth):
        kind, j = i % 4, i // 4
        xn = _rmsnorm(h, norm_mix[i], tm)
        if kind == 0:
            proj = _matmul(xn, ret_w_in[j], tm=tm, name="ret_in")
            o, s_p, s_s = _retention(proj, state_ret, j, ret_gn[j], dims)
            outs["ret_p"].append(s_p)
            outs["ret_s"].append(s_s)
            w_o = ret_w_o[j]
        elif kind == 1:
            uv = _matmul(xn, cm_w_in[j], tm=tm, act="gelu", name="cm_in")
            o, vn = _cm_gate(uv, cm_ln_g[j], cm_ln_b[j], cm_w_s[j], cm_b_s[j], dims)
            outs["cm_s"].append(vn.reshape(Bs, T, -1))
            w_o = cm_w_o[j]
        elif kind == 2:
            proj = _matmul(xn, dsa_w_in[j], tm=tm, name="dsa_in")
            o, k, v, ki = _dsa(proj, (cache_k_c, cache_v_c, cache_kidx_c), j, page_table,
                               dsa_q_norm[j], dsa_k_norm[j], dims, dsa_w_o.shape[1] // cache_k_c.shape[-1])
            outs["kc"].append(k)
            outs["vc"].append(v)
            outs["ic"].append(ki)
            w_o = dsa_w_o[j]
        else:
            o, ckv, kpe = _mla(xn, (cache_ckv_d, cache_kpe_d), j, page_table, mla_w_in[j], mla_qa_norm[j],
                               mla_kv_norm[j], mla_w_uq[j], mla_qn_nope[j], mla_qn_pe[j], mla_kpe_norm[j],
                               mla_w_uk[j], mla_w_uv[j], dims, tm)
            outs["ckv"].append(ckv)
            outs["kpe"].append(kpe)
            w_o = mla_w_o[j]
        h = _matmul(o, w_o, tm=tm, residual=h, name="mix_out")
        xn = _rmsnorm(h, norm_ffn[i], tm, out_dtype=F32)
        h = _moe(h, xn, moe_w_group[i], moe_b_group[i], moe_w_expert[i], moe_b_expert[i],
                 moe_w1, moe_w3, moe_w2, i, tm)

    KVH, HD = cache_k_c.shape[3:]
    stack_p = lambda xs, shp: jnp.stack([x[:Np].reshape((B, L) + shp) for x in xs])
    stack_s = lambda xs, shp: jnp.stack([x[Np:].reshape((Bs, T) + shp) for x in xs])
    return (h[:Np].reshape(B, L, D), h[Np:].reshape(Bs, T, D),
            jnp.stack(outs["ret_p"]), jnp.stack(outs["ret_s"]), jnp.stack(outs["cm_s"]),
            stack_p(outs["kc"], (KVH, HD)), stack_p(outs["vc"], (KVH, HD)), stack_p(outs["ic"], (cache_kidx_c.shape[-1],)),
            stack_s(outs["kc"], (KVH, HD)), stack_s(outs["vc"], (KVH, HD)), stack_s(outs["ic"], (cache_kidx_c.shape[-1],)),
            stack_p(outs["ckv"], (cache_ckv_d.shape[-1],)), stack_p(outs["kpe"], (cache_kpe_d.shape[-1],)),
            stack_s(outs["ckv"], (cache_ckv_d.shape[-1],)), stack_s(outs["kpe"], (cache_kpe_d.shape[-1],)))
```
